```python
import math
import jax, jax.numpy as jnp
from jax import lax
import numpy as np

D_MODEL = 1024
BATCH = 8
SEQ = 8192
DEPTH = 2
DEC_BATCH = 4
DEC_SEQ = 8192
PAST_LEN = 128

ATT_HEADS = 8
ATT_KV_HEADS = 2
ATT_GROUP = ATT_HEADS // ATT_KV_HEADS
HEAD_DIM = 64
WINDOW = 128
BLOCK = 128
REL_BUCKETS = 32
REL_MAX_DIST = 128
DN_HEADS = 8
DN_DK = 64
DN_DV = 64
DN_CONV = 3
DN_CHUNK = 64
D_FF = 2816
ALPHA = (2 * DEPTH) ** 0.25
BETA_INIT = (8 * DEPTH) ** -0.25
LN_EPS = 1e-5
RMS_EPS = 1e-6

ATT_Q = ATT_HEADS * HEAD_DIM
ATT_KV = ATT_KV_HEADS * HEAD_DIM
DN_K = DN_HEADS * DN_DK
DN_V = DN_HEADS * DN_DV
DN_QKV = 2 * DN_K + DN_V
IN_WIDTH = ATT_Q + 2 * ATT_KV + DN_QKV + DN_V + 4 * DN_HEADS
SPLIT_POINTS = (ATT_Q, ATT_Q + ATT_KV, ATT_Q + 2 * ATT_KV, ATT_Q + 2 * ATT_KV + DN_QKV,
                ATT_Q + 2 * ATT_KV + DN_QKV + DN_V, ATT_Q + 2 * ATT_KV + DN_QKV + DN_V + 2 * DN_HEADS)

kernel_name = 'hybrid_window_gdn_encoder'


def _layer_norm(x, g, b):
    xf = x.astype(jnp.float32)
    mu = jnp.mean(xf, -1, keepdims=True)
    var = jnp.mean(jnp.square(xf - mu), -1, keepdims=True)
    return ((xf - mu) * lax.rsqrt(var + LN_EPS) * g.astype(jnp.float32) + b.astype(jnp.float32)).astype(x.dtype)


def _swiglu(x, wg, wu, wd):
    return (jax.nn.silu(x @ wg) * (x @ wu)) @ wd


def _t5_buckets(rel):
    half = REL_BUCKETS // 2
    ret = (rel > 0).astype(np.int32) * half
    n = np.abs(rel)
    max_exact = half // 2
    large = max_exact + (np.log(np.maximum(n, 1) / max_exact) / np.log(REL_MAX_DIST / max_exact)
                         * (half - max_exact)).astype(np.int32)
    large = np.minimum(large, half - 1)
    return (ret + np.where(n < max_exact, n, large)).astype(np.int32)


def _band_bias_and_mask(rel_bias, s):
    nb = s // BLOCK
    i = np.arange(BLOCK)[:, None]
    j = np.arange(3 * BLOCK)[None, :]
    rel = j - BLOCK - i
    bias = rel_bias.astype(jnp.float32)[_t5_buckets(rel)]
    bias = jnp.transpose(bias, (2, 0, 1)).reshape(ATT_KV_HEADS, ATT_GROUP, BLOCK, 3 * BLOCK)
    key_pos = np.arange(nb)[:, None] * BLOCK + np.arange(3 * BLOCK)[None, :] - BLOCK
    inside = (key_pos >= 0) & (key_pos < s)
    valid = (np.abs(rel) <= WINDOW)[None] & inside[:, None, :]
    return bias, jnp.asarray(valid)


def _window_attention(q, k, v, sink, bias, valid):
    b, s, _ = q.shape
    nb = s // BLOCK
    qb = q.astype(jnp.float32).reshape(b, nb, BLOCK, ATT_KV_HEADS, ATT_GROUP, HEAD_DIM) * HEAD_DIM ** -0.5

    def windows(t):
        tp = jnp.pad(t.reshape(b, s, ATT_KV_HEADS, HEAD_DIM), ((0, 0), (BLOCK, BLOCK), (0, 0), (0, 0)))
        tp = tp.reshape(b, nb + 2, BLOCK, ATT_KV_HEADS, HEAD_DIM)
        return jnp.concatenate([tp[:, :-2], tp[:, 1:-1], tp[:, 2:]], axis=2).astype(jnp.float32)

    kw = windows(k)
    vw = windows(v)
    logits = jnp.einsum('bnqkgd,bnjkd->bnkgqj', qb, kw) + bias
    logits = jnp.where(valid[None, :, None, None], logits, -jnp.inf)
    sk = sink.astype(jnp.float32).reshape(ATT_KV_HEADS, ATT_GROUP, 1)
    m = jnp.maximum(jnp.max(logits, -1), sk)
    p = jnp.exp(logits - m[..., None])
    denom = jnp.sum(p, -1) + jnp.exp(sk - m)
    out = jnp.einsum('bnkgqj,bnjkd->bnqkgd', p / denom[..., None], vw)
    return out.reshape(b, s, ATT_Q).astype(q.dtype)


def _l2norm(t):
    return t * lax.rsqrt(jnp.sum(t * t, -1, keepdims=True) + 1e-6)


def _chunk_gated_delta_rule(q, k, v, log_g, beta):
    n, s, h, dk = q.shape
    dv = v.shape[-1]
    nc = s // DN_CHUNK

    def blk(t):
        return jnp.moveaxis(t.reshape(n, nc, DN_CHUNK, h, -1), 3, 1)

    qc, kc, vc = blk(q), blk(k), blk(v)
    gc = blk(log_g[..., None])[..., 0]
    bc = blk(beta[..., None])[..., 0]
    G = jnp.cumsum(gc, -1)
    idx = np.arange(DN_CHUNK)
    incl = jnp.asarray(idx[:, None] >= idx[None, :])
    strict = jnp.asarray(idx[:, None] > idx[None, :])
    decay = jnp.exp(jnp.where(incl, G[..., :, None] - G[..., None, :], -jnp.inf))
    a = jnp.where(strict, bc[..., :, None] * jnp.einsum('nhcid,nhcjd->nhcij', kc, kc) * decay, 0.0)
    eye = jnp.eye(DN_CHUNK, dtype=jnp.float32)
    t_inv = lax.linalg.triangular_solve(eye + a, jnp.broadcast_to(eye, a.shape),
                                        left_side=True, lower=True, unit_diagonal=True)
    eg = jnp.exp(G)
    u0 = t_inv @ (bc[..., None] * vc)
    w = t_inv @ ((bc * eg)[..., None] * kc)
    att = jnp.einsum('nhcid,nhcjd->nhcij', qc, kc) * decay
    qg = qc * eg[..., None]
    g_last = G[..., -1:]
    kd = kc * jnp.exp(g_last - G)[..., None]
    gl = jnp.exp(g_last)[..., 0]
    xs = (jnp.moveaxis(qg, 2, 0), jnp.moveaxis(w, 2, 0), jnp.moveaxis(u0, 2, 0),
          jnp.moveaxis(kd, 2, 0), jnp.moveaxis(att, 2, 0), jnp.moveaxis(gl, 2, 0))

    def step(state, inp):
        qg_c, w_c, u0_c, kd_c, att_c, gl_c = inp
        u = u0_c - w_c @ state
        o = qg_c @ state + att_c @ u
        state = state * gl_c[..., None, None] + jnp.swapaxes(kd_c, -1, -2) @ u
        return state, o

    s0 = jnp.zeros((n, h, dk, dv), jnp.float32)
    _, o = lax.scan(step, s0, xs)
    return jnp.transpose(o, (1, 0, 3, 2, 4)).reshape(n, s, h, dv)


def _bidir_gated_deltanet(qkv, z, a, bt, conv_w, a_log, dt_bias, norm_w):
    b, s, _ = qkv.shape
    pad = DN_CONV // 2
    xp = jnp.pad(qkv, ((0, 0), (pad, pad), (0, 0)))
    c = xp[:, 0:s] * conv_w[0]
    for j in range(1, DN_CONV):
        c = c + xp[:, j:j + s] * conv_w[j]
    c = jax.nn.silu(c).astype(jnp.float32)
    q, k, v = jnp.split(c, [DN_K, 2 * DN_K], axis=-1)
    q = _l2norm(q.reshape(b, s, DN_HEADS, DN_DK)) * DN_DK ** -0.5
    k = _l2norm(k.reshape(b, s, DN_HEADS, DN_DK))
    v = v.reshape(b, s, DN_HEADS, DN_DV)
    a = a.astype(jnp.float32).reshape(b, s, 2, DN_HEADS)
    log_g = -jnp.exp(a_log.astype(jnp.float32)) * jax.nn.softplus(a + dt_bias.astype(jnp.float32))
    beta = jax.nn.sigmoid(bt.astype(jnp.float32).reshape(b, s, 2, DN_HEADS))

    def both(t):
        return jnp.concatenate([t, jnp.flip(t, 1)], axis=0)

    o = _chunk_gated_delta_rule(both(q), both(k), both(v),
                                jnp.concatenate([log_g[:, :, 0], jnp.flip(log_g[:, :, 1], 1)], axis=0),
                                jnp.concatenate([beta[:, :, 0], jnp.flip(beta[:, :, 1], 1)], axis=0))
    o = o[:b] + jnp.flip(o[b:], 1)
    o = o * lax.rsqrt(jnp.mean(o * o, -1, keepdims=True) + RMS_EPS) * norm_w.astype(jnp.float32)
    o = o * jax.nn.silu(z.astype(jnp.float32).reshape(b, s, DN_HEADS, DN_DV))
    return o.reshape(b, s, DN_V).astype(qkv.dtype)


def _mixer(x, w_in, sink, bias, valid, conv_w, a_log, dt_bias, norm_w, w_att_out, w_dn_out, w_gate, b_gate, w_o):
    proj = x @ w_in
    qa, ka, va, qkv, z, a, bt = jnp.split(proj, SPLIT_POINTS, axis=-1)
    y_att = _window_attention(qa, ka, va, sink, bias, valid) @ w_att_out
    y_dn = _bidir_gated_deltanet(qkv, z, a, bt, conv_w, a_log, dt_bias, norm_w) @ w_dn_out
    g_att, g_dn = jnp.split(jax.nn.sigmoid(x @ w_gate + b_gate), 2, axis=-1)
    return (g_att * y_att + g_dn * y_dn) @ w_o


def _trunk(x, ln_g, ln_b, ffn_w_gate, ffn_w_up, ffn_w_down, w_in, attn_sink, rel_bias,
           dn_conv, dn_a_log, dn_dt_bias, dn_norm_w, w_att_out, w_dn_out, w_gate, b_gate, w_o):
    bias, valid = _band_bias_and_mask(rel_bias, x.shape[1])
    for l in range(DEPTH):
        x = _layer_norm(ALPHA * x + 0.5 * _swiglu(x, ffn_w_gate[l, 0], ffn_w_up[l, 0], ffn_w_down[l, 0]),
                        ln_g[l, 0], ln_b[l, 0])
        x = _layer_norm(ALPHA * x + _mixer(x, w_in[l], attn_sink[l], bias, valid, dn_conv[l], dn_a_log[l],
                                           dn_dt_bias[l], dn_norm_w[l], w_att_out[l], w_dn_out[l],
                                           w_gate[l], b_gate[l], w_o[l]),
                        ln_g[l, 1], ln_b[l, 1])
        x = _layer_norm(ALPHA * x + 0.5 * _swiglu(x, ffn_w_gate[l, 1], ffn_w_up[l, 1], ffn_w_down[l, 1]),
                        ln_g[l, 2], ln_b[l, 2])
    return x


def setup_inputs(seed: int = 0) -> dict:
    key = jax.random.key(seed)
    ks = jax.random.split(key, 20)
    f32 = jnp.float32

    def nrm(k, shape, scale):
        return jax.random.normal(k, shape, f32) * scale

    dt = jnp.exp(jax.random.uniform(ks[12], (DEPTH, 2, DN_HEADS), f32) * (math.log(0.1) - math.log(0.001))
                 + math.log(0.001))
    return {
        'x_prompt': nrm(ks[0], (BATCH, SEQ, D_MODEL), 1.0),
        'x_sample': nrm(ks[1], (DEC_BATCH, DEC_SEQ, D_MODEL), 1.0),
        'ln_g': 1.0 + nrm(ks[2], (DEPTH, 3, D_MODEL), 0.02),
        'ln_b': nrm(ks[3], (DEPTH, 3, D_MODEL), 0.02),
        'ffn_w_gate': nrm(ks[4], (DEPTH, 2, D_MODEL, D_FF), D_MODEL ** -0.5),
        'ffn_w_up': nrm(ks[5], (DEPTH, 2, D_MODEL, D_FF), D_MODEL ** -0.5),
        'ffn_w_down': nrm(ks[6], (DEPTH, 2, D_FF, D_MODEL), D_FF ** -0.5 * BETA_INIT),
        'w_in': nrm(ks[7], (DEPTH, D_MODEL, IN_WIDTH), D_MODEL ** -0.5),
        'attn_sink': nrm(ks[8], (DEPTH, ATT_HEADS), 1.0),
        'rel_bias': nrm(ks[9], (REL_BUCKETS, ATT_HEADS), 0.5),
        'dn_conv': nrm(ks[10], (DEPTH, DN_CONV, DN_QKV), DN_CONV ** -0.5),
        'dn_a_log': jnp.log(jax.random.uniform(ks[11], (DEPTH, 2, DN_HEADS), f32, 1.0, 16.0)),
        'dn_dt_bias': dt + jnp.log(-jnp.expm1(-dt)),
        'dn_norm_w': 1.0 + nrm(ks[13], (DEPTH, DN_DV), 0.02),
        'w_att_out': nrm(ks[14], (DEPTH, ATT_Q, D_MODEL), ATT_Q ** -0.5),
        'w_dn_out': nrm(ks[15], (DEPTH, DN_V, D_MODEL), DN_V ** -0.5),
        'w_gate': nrm(ks[16], (DEPTH, D_MODEL, 2 * D_MODEL), D_MODEL ** -0.5),
        'b_gate': nrm(ks[17], (DEPTH, 2 * D_MODEL), 0.02),
        'w_o': nrm(ks[18], (DEPTH, D_MODEL, D_MODEL), D_MODEL ** -0.5 * BETA_INIT),
    }


def reference(x_prompt, x_sample, ln_g, ln_b, ffn_w_gate, ffn_w_up, ffn_w_down, w_in, attn_sink, rel_bias,
              dn_conv, dn_a_log, dn_dt_bias, dn_norm_w, w_att_out, w_dn_out, w_gate, b_gate, w_o):
    y_prompt = _trunk(x_prompt, ln_g, ln_b, ffn_w_gate, ffn_w_up, ffn_w_down, w_in, attn_sink, rel_bias,
                      dn_conv, dn_a_log, dn_dt_bias, dn_norm_w, w_att_out, w_dn_out, w_gate, b_gate, w_o)
    y_sample = _trunk(x_sample, ln_g, ln_b, ffn_w_gate, ffn_w_up, ffn_w_down, w_in, attn_sink, rel_bias,
                      dn_conv, dn_a_log, dn_dt_bias, dn_norm_w, w_att_out, w_dn_out, w_gate, b_gate, w_o)
    return (y_prompt, y_sample)
```

```python
import functools
import math

import numpy as np
import jax
import jax.numpy as jnp
from jax import lax
from jax.experimental import pallas as pl
from jax.experimental.pallas import tpu as pltpu

F32 = jnp.float32
BF16 = jnp.bfloat16

D_MODEL = 1024
DEPTH = 2
ATT_HEADS = 8
ATT_KV_HEADS = 2
ATT_GROUP = ATT_HEADS // ATT_KV_HEADS
HEAD_DIM = 64
WINDOW = 128
BLOCK = 128
REL_BUCKETS = 32
REL_MAX_DIST = 128
DN_HEADS = 8
DN_DK = 64
DN_DV = 64
DN_CONV = 3
D_FF = 2816
ALPHA = (2 * DEPTH) ** 0.25
LN_EPS = 1e-5
RMS_EPS = 1e-6
L2_EPS = 1e-6

ATT_Q = ATT_HEADS * HEAD_DIM
ATT_KV = ATT_KV_HEADS * HEAD_DIM
DN_K = DN_HEADS * DN_DK
DN_V = DN_HEADS * DN_DV
DN_QKV = 2 * DN_K + DN_V

LANES = 128
SUBLANES = 8
MXU_DIM = 256
VMEM_LIMIT_BYTES = 56 * 1024 * 1024

FFN_ROWS = 512
FFN_COLS = 256
PROJ_ROWS = 512
ATT_ROWS = 512
GDN_ROWS = 256
GDN_CHUNK = 64
MIX_ROWS = 512
MIX_COLS = 256
GB_WIDTH = LANES
MASKED = -1e30


def _const_spec(shape):
    nd = len(shape)
    return pl.BlockSpec(shape, lambda *_: (0,) * nd, pipeline_mode=pl.Buffered(1))


def _params(n_axes):
    return pltpu.CompilerParams(dimension_semantics=("arbitrary",) * n_axes,
                                vmem_limit_bytes=VMEM_LIMIT_BYTES)


def _layer_norm_rows(r, g, b):
    mu = jnp.mean(r, axis=-1, keepdims=True)
    d = r - mu
    var = jnp.mean(d * d, axis=-1, keepdims=True)
    return d * lax.rsqrt(var + LN_EPS) * g + b


def _sigmoid(x):
    return 1.0 / (1.0 + jnp.exp(-x))


def _head_sums(sq, ones_ref):
    hi = sq.astype(BF16)
    lo = (sq - hi.astype(F32)).astype(BF16)
    ones = ones_ref[...]
    parts = []
    for c in range(0, sq.shape[1], MXU_DIM):
        parts.append(jnp.dot(hi[:, c:c + MXU_DIM], ones, preferred_element_type=F32)
                     + jnp.dot(lo[:, c:c + MXU_DIM], ones, preferred_element_type=F32))
    return jnp.concatenate(parts, axis=1)


def _ffn_kernel(x_ref, wg_ref, wu_ref, wd_ref, g_ref, b_ref, o_ref, h_scr):
    x = x_ref[...]
    xb = x.astype(BF16)
    for c in range(0, D_FF, FFN_COLS):
        gate = jnp.dot(xb, wg_ref[:, c:c + FFN_COLS], preferred_element_type=F32)
        up = jnp.dot(xb, wu_ref[:, c:c + FFN_COLS], preferred_element_type=F32)
        h_scr[:, c:c + FFN_COLS] = (gate * _sigmoid(gate) * up).astype(BF16)
    y = jnp.dot(h_scr[...], wd_ref[...], preferred_element_type=F32)
    o_ref[...] = _layer_norm_rows(ALPHA * x + 0.5 * y, g_ref[...], b_ref[...])


def _ffn_ln(x2d, wg, wu, wd, g, b):
    n = x2d.shape[0]
    rows = min(FFN_ROWS, n)
    row_spec = pl.BlockSpec((rows, D_MODEL), lambda i: (i, 0))
    return pl.pallas_call(
        _ffn_kernel,
        grid=(n // rows,),
        in_specs=[row_spec, _const_spec(wg.shape), _const_spec(wu.shape), _const_spec(wd.shape),
                  _const_spec(g.shape), _const_spec(b.shape)],
        out_specs=row_spec,
        out_shape=jax.ShapeDtypeStruct((n, D_MODEL), F32),
        scratch_shapes=[pltpu.VMEM((rows, D_FF), BF16)],
        compiler_params=_params(1),
        name="ffn_ln",
    )(x2d, wg, wu, wd, g, b)


def _proj_kernel(xp_ref, x_ref, xn_ref, watt_ref, wdn_ref, wz_ref, wab_ref, conv_ref,
                 nexpa_ref, dtb_ref, ones_ref,
                 qa_ref, ka_ref, va_ref, qn_ref, kn_ref, vn_ref, z_ref, gb_ref):
    t = pl.program_id(1)
    nt = pl.num_programs(1)
    rows = x_ref.shape[0]
    x = x_ref[...]
    xb = x.astype(BF16)

    att = jnp.dot(xb, watt_ref[...], preferred_element_type=F32)
    qa_ref[...] = (att[:, :ATT_Q] * HEAD_DIM ** -0.5).astype(BF16)
    ka_ref[...] = att[:, ATT_Q:ATT_Q + ATT_KV].astype(BF16)
    va_ref[...] = att[:, ATT_Q + ATT_KV:].astype(BF16)

    z_ref[...] = jnp.dot(xb, wz_ref[...], preferred_element_type=F32).astype(BF16)

    ab = jnp.dot(xb, wab_ref[...], preferred_element_type=F32)
    sp_in = ab + dtb_ref[...]
    softplus = jnp.maximum(sp_in, 0.0) + jnp.log1p(jnp.exp(-jnp.abs(sp_in)))
    lane = lax.broadcasted_iota(jnp.int32, ab.shape, 1)
    gb_ref[...] = jnp.where(lane < 2 * DN_HEADS, nexpa_ref[...] * softplus,
                            jnp.where(lane < 4 * DN_HEADS, _sigmoid(ab), 0.0))

    prev_ok = (t > 0).astype(F32)
    next_ok = (t < nt - 1).astype(F32)
    ext = jnp.concatenate([xp_ref[...] * prev_ok, x, xn_ref[...] * next_ok], axis=0).astype(BF16)
    p = jnp.dot(ext, wdn_ref[...], preferred_element_type=F32)
    n_ext = rows + 2 * SUBLANES
    conv = conv_ref[...]
    c = (pltpu.roll(p, 1, 0) * conv[0:1, :] + p * conv[1:2, :]
         + pltpu.roll(p, n_ext - 1, 0) * conv[2:3, :])
    c = c[SUBLANES:SUBLANES + rows, :]
    c = c * _sigmoid(c)
    q = c[:, :DN_K]
    k = c[:, DN_K:2 * DN_K]
    qn_ref[...] = (q * lax.rsqrt(_head_sums(q * q, ones_ref) + L2_EPS) * DN_DK ** -0.5).astype(BF16)
    kn_ref[...] = (k * lax.rsqrt(_head_sums(k * k, ones_ref) + L2_EPS)).astype(BF16)
    vn_ref[...] = c[:, 2 * DN_K:].astype(BF16)


def _proj(x3d, watt, wdn, wz, wab, conv, nexpa, dtb, ones):
    bsz, s, _ = x3d.shape
    rows = min(PROJ_ROWS, s)
    nblk = s // SUBLANES
    rb = rows // SUBLANES

    def tile(width):
        return pl.BlockSpec((None, rows, width), lambda b, t: (b, t, 0))

    prev_spec = pl.BlockSpec((None, SUBLANES, D_MODEL),
                             lambda b, t: (b, jnp.maximum(t * rb - 1, 0), 0))
    next_spec = pl.BlockSpec((None, SUBLANES, D_MODEL),
                             lambda b, t: (b, jnp.minimum((t + 1) * rb, nblk - 1), 0))
    widths = (ATT_Q, ATT_KV, ATT_KV, DN_K, DN_K, DN_V, DN_V)
    out_shape = [jax.ShapeDtypeStruct((bsz, s, w), BF16) for w in widths]
    out_shape.append(jax.ShapeDtypeStruct((bsz, s, GB_WIDTH), F32))
    out_specs = [tile(w) for w in widths] + [tile(GB_WIDTH)]
    consts = (watt, wdn, wz, wab, conv, nexpa, dtb, ones)
    return pl.pallas_call(
        _proj_kernel,
        grid=(bsz, s // rows),
        in_specs=[prev_spec, tile(D_MODEL), next_spec] + [_const_spec(c.shape) for c in consts],
        out_specs=out_specs,
        out_shape=out_shape,
        compiler_params=_params(2),
        name="in_proj",
    )(x3d, x3d, x3d, *consts)


def _attn_kernel(sink_ref, q_ref, kp_ref, k_ref, kn_ref, vp_ref, v_ref, vn_ref, bias_ref, o_ref):
    t = pl.program_id(1)
    nt = pl.num_programs(1)
    rows = q_ref.shape[0]
    nblk = rows // BLOCK
    kcat = jnp.concatenate([kp_ref[...], k_ref[...], kn_ref[...]], axis=0)
    vcat = jnp.concatenate([vp_ref[...], v_ref[...], vn_ref[...]], axis=0)
    col = lax.broadcasted_iota(jnp.int32, (BLOCK, 3 * BLOCK), 1)
    first_tile = t == 0
    last_tile = t == nt - 1
    for j in range(nblk):
        r0 = j * BLOCK
        edge = None
        if j == 0:
            edge = jnp.logical_and(first_tile, col < BLOCK)
        if j == nblk - 1:
            hi_edge = jnp.logical_and(last_tile, col >= 2 * BLOCK)
            edge = hi_edge if edge is None else jnp.logical_or(edge, hi_edge)
        for h in range(ATT_HEADS):
            kh = h // ATT_GROUP
            q = q_ref[r0:r0 + BLOCK, h * HEAD_DIM:(h + 1) * HEAD_DIM]
            kw = kcat[r0:r0 + 3 * BLOCK, kh * HEAD_DIM:(kh + 1) * HEAD_DIM]
            vw = vcat[r0:r0 + 3 * BLOCK, kh * HEAD_DIM:(kh + 1) * HEAD_DIM]
            s = lax.dot_general(q, kw, (((1,), (1,)), ((), ())), preferred_element_type=F32)
            s = s + bias_ref[h]
            if edge is not None:
                s = jnp.where(edge, MASKED, s)
            sink = sink_ref[h]
            m = jnp.maximum(jnp.max(s, axis=-1, keepdims=True), sink)
            p = jnp.exp(s - m)
            denom = jnp.sum(p, axis=-1, keepdims=True) + jnp.exp(sink - m)
            o = jnp.dot(p.astype(BF16), vw, preferred_element_type=F32) / denom
            o_ref[r0:r0 + BLOCK, h * HEAD_DIM:(h + 1) * HEAD_DIM] = o.astype(BF16)


def _attention(qa, ka, va, bias, sink):
    bsz, s, _ = qa.shape
    rows = min(ATT_ROWS, s)
    rb = rows // BLOCK
    nblk = s // BLOCK

    def tile(width):
        return pl.BlockSpec((None, rows, width), lambda b, t: (b, t, 0))

    prev_spec = pl.BlockSpec((None, BLOCK, ATT_KV), lambda b, t: (b, jnp.maximum(t * rb - 1, 0), 0))
    next_spec = pl.BlockSpec((None, BLOCK, ATT_KV),
                             lambda b, t: (b, jnp.minimum((t + 1) * rb, nblk - 1), 0))
    return pl.pallas_call(
        _attn_kernel,
        grid=(bsz, s // rows),
        in_specs=[pl.BlockSpec(memory_space=pltpu.SMEM),
                  tile(ATT_Q), prev_spec, tile(ATT_KV), next_spec,
                  prev_spec, tile(ATT_KV), next_spec, _const_spec(bias.shape)],
        out_specs=tile(ATT_Q),
        out_shape=jax.ShapeDtypeStruct((bsz, s, ATT_Q), BF16),
        compiler_params=_params(2),
        name="window_attn",
    )(sink, qa, ka, ka, ka, va, va, va, bias)


def _segmented_cumsum(x, axis, seg, reverse):
    n = x.shape[axis]
    pos = lax.broadcasted_iota(jnp.int32, x.shape, axis) % seg
    step = 1
    while step < seg:
        if reverse:
            shifted = pltpu.roll(x, n - step, axis)
            keep = pos < seg - step
        else:
            shifted = pltpu.roll(x, step, axis)
            keep = pos >= step
        x = x + jnp.where(keep, shifted, 0.0)
        step *= 2
    return x


def _gdn_kernel(q_ref, k_ref, v_ref, gb_ref, gbt_ref, o_ref, s_scr, *, reverse):
    t = pl.program_id(1)
    rows = q_ref.shape[0]
    chunk = GDN_CHUNK
    nchunk = rows // chunk

    @pl.when(t == 0)
    def _():
        s_scr[...] = jnp.zeros_like(s_scr)

    gb = gb_ref[...]
    g_col = _segmented_cumsum(gb, 0, chunk, reverse)
    g_row = _segmented_cumsum(gbt_ref[...], 1, chunk, reverse)
    dcol = DN_HEADS if reverse else 0
    bcol = 2 * DN_HEADS + dcol

    ri = lax.broadcasted_iota(jnp.int32, (chunk, chunk), 0)
    ci = lax.broadcasted_iota(jnp.int32, (chunk, chunk), 1)
    incl = (ri <= ci) if reverse else (ri >= ci)
    strict = (ri < ci) if reverse else (ri > ci)
    eye = (ri == ci).astype(F32)
    last = 0 if reverse else chunk - 1
    order = range(nchunk - 1, -1, -1) if reverse else range(nchunk)
    nt_dims = (((1,), (1,)), ((), ()))
    tn_dims = (((0,), (0,)), ((), ()))

    for c in order:
        r0 = c * chunk
        for h in range(DN_HEADS):
            lanes = slice(h * DN_DK, (h + 1) * DN_DK)
            q = q_ref[r0:r0 + chunk, lanes]
            k = k_ref[r0:r0 + chunk, lanes]
            v = v_ref[r0:r0 + chunk, lanes]
            qf = q.astype(F32)
            kf = k.astype(F32)
            vf = v.astype(F32)
            gi = g_col[r0:r0 + chunk, dcol + h:dcol + h + 1]
            gj = g_row[dcol + h:dcol + h + 1, r0:r0 + chunk]
            beta = gb[r0:r0 + chunk, bcol + h:bcol + h + 1]
            g_end = gi[last:last + 1, :]
            decay = jnp.where(incl, jnp.exp(jnp.where(incl, gi - gj, 0.0)), 0.0)
            kk = lax.dot_general(k, k, nt_dims, preferred_element_type=F32)
            qk = lax.dot_general(q, k, nt_dims, preferred_element_type=F32)
            neg_a = jnp.where(strict, -(beta * kk * decay), 0.0)
            inv = eye + neg_a
            power = neg_a
            step = 2
            while step < chunk:
                pb = power.astype(BF16)
                power = jnp.dot(pb, pb, preferred_element_type=F32)
                inv = inv + jnp.dot(inv.astype(BF16), power.astype(BF16), preferred_element_type=F32)
                step *= 2
            eg = jnp.exp(gi)
            rhs = jnp.concatenate([(beta * eg) * kf, beta * vf], axis=1).astype(BF16)
            wu = jnp.dot(inv.astype(BF16), rhs, preferred_element_type=F32).astype(BF16)
            att = (qk * decay).astype(BF16)
            aw = jnp.dot(att, wu, preferred_element_type=F32)
            kd = (kf * jnp.exp(g_end - gi)).astype(BF16)
            mb = lax.dot_general(kd, wu, tn_dims, preferred_element_type=F32)
            state = s_scr[h]
            sb = state.astype(BF16)
            q_eff = (qf * eg - aw[:, :DN_DK]).astype(BF16)
            o = jnp.dot(q_eff, sb, preferred_element_type=F32) + aw[:, DN_DK:]
            s_scr[h] = (jnp.exp(g_end) * state
                        - jnp.dot(mb[:, :DN_DK].astype(BF16), sb, preferred_element_type=F32)
                        + mb[:, DN_DK:])
            o_ref[r0:r0 + chunk, lanes] = o


def _gdn_scan(qn, kn, vn, gb, gbt, reverse):
    bsz, s, _ = qn.shape
    rows = min(GDN_ROWS, s)
    nt = s // rows

    def tmap(t):
        return nt - 1 - t if reverse else t

    def tile(width):
        return pl.BlockSpec((None, rows, width), lambda b, t: (b, tmap(t), 0))

    gbt_spec = pl.BlockSpec((None, gbt.shape[1], rows), lambda b, t: (b, 0, tmap(t)))
    return pl.pallas_call(
        functools.partial(_gdn_kernel, reverse=reverse),
        grid=(bsz, nt),
        in_specs=[tile(DN_K), tile(DN_K), tile(DN_V), tile(GB_WIDTH), gbt_spec],
        out_specs=tile(DN_V),
        out_shape=jax.ShapeDtypeStruct((bsz, s, DN_V), F32),
        scratch_shapes=[pltpu.VMEM((DN_HEADS, DN_DK, DN_DV), F32)],
        compiler_params=_params(2),
        name="gdn_bwd" if reverse else "gdn_fwd",
    )(qn, kn, vn, gb, gbt)


def _mix_kernel(x_ref, ao_ref, of_ref, ob_ref, z_ref, nw_ref, watt_ref, wdn_ref, wgate_ref,
                bgate_ref, wo_ref, g_ref, b_ref, ones_ref, o_ref, mix_scr):
    x = x_ref[...]
    xb = x.astype(BF16)
    o = of_ref[...] + ob_ref[...]
    ms = _head_sums(o * o, ones_ref) * (1.0 / DN_DV)
    z = z_ref[...].astype(F32)
    dn = (o * lax.rsqrt(ms + RMS_EPS) * nw_ref[...] * (z * _sigmoid(z))).astype(BF16)
    ao = ao_ref[...]
    for c in range(0, D_MODEL, MIX_COLS):
        cols = slice(c, c + MIX_COLS)
        gcols = slice(D_MODEL + c, D_MODEL + c + MIX_COLS)
        y_att = jnp.dot(ao, watt_ref[:, cols], preferred_element_type=F32)
        y_dn = jnp.dot(dn, wdn_ref[:, cols], preferred_element_type=F32)
        g_att = _sigmoid(jnp.dot(xb, wgate_ref[:, cols], preferred_element_type=F32) + bgate_ref[:, cols])
        g_dn = _sigmoid(jnp.dot(xb, wgate_ref[:, gcols], preferred_element_type=F32) + bgate_ref[:, gcols])
        mix_scr[:, cols] = (g_att * y_att + g_dn * y_dn).astype(BF16)
    m = jnp.dot(mix_scr[...], wo_ref[...], preferred_element_type=F32)
    o_ref[...] = _layer_norm_rows(ALPHA * x + m, g_ref[...], b_ref[...])


def _mix(x2d, ao, of, ob, z, nw, watt, wdn, wgate, bgate, wo, g, b, ones):
    n = x2d.shape[0]
    rows = min(MIX_ROWS, n)

    def tile(width):
        return pl.BlockSpec((rows, width), lambda i: (i, 0))

    consts = (nw, watt, wdn, wgate, bgate, wo, g, b, ones)
    return pl.pallas_call(
        _mix_kernel,
        grid=(n // rows,),
        in_specs=[tile(D_MODEL), tile(ATT_Q), tile(DN_V), tile(DN_V), tile(DN_V)]
                 + [_const_spec(c.shape) for c in consts],
        out_specs=tile(D_MODEL),
        out_shape=jax.ShapeDtypeStruct((n, D_MODEL), F32),
        scratch_shapes=[pltpu.VMEM((rows, D_MODEL), BF16)],
        compiler_params=_params(1),
        name="mix_out",
    )(x2d, ao, of, ob, z, *consts)


def _t5_buckets(rel):
    half = REL_BUCKETS // 2
    ret = (rel > 0).astype(np.int32) * half
    n = np.abs(rel)
    max_exact = half // 2
    large = max_exact + (np.log(np.maximum(n, 1) / max_exact) / np.log(REL_MAX_DIST / max_exact)
                         * (half - max_exact)).astype(np.int32)
    large = np.minimum(large, half - 1)
    return (ret + np.where(n < max_exact, n, large)).astype(np.int32)


def _band_bias(rel_bias):
    i = np.arange(BLOCK)[:, None]
    j = np.arange(3 * BLOCK)[None, :]
    rel = j - BLOCK - i
    bias = jnp.transpose(rel_bias.astype(F32)[_t5_buckets(rel)], (2, 0, 1))
    return jnp.where(jnp.asarray(np.abs(rel) <= WINDOW)[None], bias, MASKED)


def _head_block_ones():
    idx = np.arange(MXU_DIM) // DN_DK
    return jnp.asarray(idx[:, None] == idx[None, :], dtype=BF16)


def _row(v):
    return v.reshape(1, -1).astype(F32)


def _pad_lanes(v, width):
    v = v.reshape(1, -1).astype(F32)
    return jnp.pad(v, ((0, 0), (0, width - v.shape[1])))


def _trunk(x, layers, bias, ones):
    bsz, s, _ = x.shape
    x2d = x.reshape(bsz * s, D_MODEL)
    for lp in layers:
        x2d = _ffn_ln(x2d, *lp["ffn0"])
        x3d = x2d.reshape(bsz, s, D_MODEL)
        qa, ka, va, qn, kn, vn, z, gb = _proj(x3d, *lp["proj"], ones)
        ao = _attention(qa, ka, va, bias, lp["sink"])
        gbt = jnp.swapaxes(gb[:, :, :4 * DN_HEADS], 1, 2)
        o_fwd = _gdn_scan(qn, kn, vn, gb, gbt, reverse=False)
        o_bwd = _gdn_scan(qn, kn, vn, gb, gbt, reverse=True)
        flat = lambda a: a.reshape(bsz * s, a.shape[-1])
        x2d = _mix(x2d, flat(ao), flat(o_fwd), flat(o_bwd), flat(z), *lp["mix"], ones)
        x2d = _ffn_ln(x2d, *lp["ffn1"])
    return x2d.reshape(bsz, s, D_MODEL)


def kernel(x_prompt, x_sample, ln_g, ln_b, ffn_w_gate, ffn_w_up, ffn_w_down, w_in, attn_sink, rel_bias,
           dn_conv, dn_a_log, dn_dt_bias, dn_norm_w, w_att_out, w_dn_out, w_gate, b_gate, w_o):
    bias = _band_bias(rel_bias)
    ones = _head_block_ones()
    c_att = ATT_Q + 2 * ATT_KV
    c_dn = c_att + DN_QKV
    c_z = c_dn + DN_V
    layers = []
    for l in range(DEPTH):
        wi = w_in[l]
        wab = jnp.pad(wi[:, c_z:], ((0, 0), (0, GB_WIDTH - 4 * DN_HEADS))).astype(BF16)
        nexpa = _pad_lanes(-jnp.exp(dn_a_log[l].astype(F32)), GB_WIDTH)
        dtb = _pad_lanes(dn_dt_bias[l], GB_WIDTH)
        layers.append({
            "ffn0": (ffn_w_gate[l, 0].astype(BF16), ffn_w_up[l, 0].astype(BF16),
                     ffn_w_down[l, 0].astype(BF16), _row(ln_g[l, 0]), _row(ln_b[l, 0])),
            "ffn1": (ffn_w_gate[l, 1].astype(BF16), ffn_w_up[l, 1].astype(BF16),
                     ffn_w_down[l, 1].astype(BF16), _row(ln_g[l, 2]), _row(ln_b[l, 2])),
            "proj": (wi[:, :c_att].astype(BF16), wi[:, c_att:c_dn].astype(BF16),
                     wi[:, c_dn:c_z].astype(BF16), wab, dn_conv[l].astype(F32), nexpa, dtb),
            "sink": attn_sink[l].astype(F32),
            "mix": (_row(jnp.tile(dn_norm_w[l], DN_HEADS)), w_att_out[l].astype(BF16),
                    w_dn_out[l].astype(BF16), w_gate[l].astype(BF16), _row(b_gate[l]),
                    w_o[l].astype(BF16), _row(ln_g[l, 1]), _row(ln_b[l, 1])),
        })
    y_prompt = _trunk(x_prompt, layers, bias, ones)
    y_sample = _trunk(x_sample, layers, bias, ones)
    return (y_prompt, y_sample)
```

```python
import functools
import math

import numpy as np
import jax
import jax.numpy as jnp
from jax import lax
from jax.experimental import pallas as pl
from jax.experimental.pallas import tpu as pltpu

F32 = jnp.float32
BF16 = jnp.bfloat16

D_MODEL = 1024
DEPTH = 2
ATT_HEADS = 8
ATT_KV_HEADS = 2
ATT_GROUP = ATT_HEADS // ATT_KV_HEADS
HEAD_DIM = 64
WINDOW = 128
BLOCK = 128
REL_BUCKETS = 32
REL_MAX_DIST = 128
DN_HEADS = 8
DN_DK = 64
DN_DV = 64
DN_CONV = 3
D_FF = 2816
ALPHA = (2 * DEPTH) ** 0.25
LN_EPS = 1e-5
RMS_EPS = 1e-6
L2_EPS = 1e-6

ATT_Q = ATT_HEADS * HEAD_DIM
ATT_KV = ATT_KV_HEADS * HEAD_DIM
DN_K = DN_HEADS * DN_DK
DN_V = DN_HEADS * DN_DV
DN_QKV = 2 * DN_K + DN_V

LANES = 128
SUBLANES = 8
MXU_DIM = 256
VMEM_LIMIT_BYTES = 56 * 1024 * 1024

FFN_ROWS = 512
FFN_COLS = 256
PROJ_ROWS = 512
ATT_ROWS = 512
GDN_ROWS = 256
GDN_CHUNK = 64
MIX_ROWS = 512
MIX_COLS = 256
GDN_GROUP = 4
GDN_PAIR = 2
GB_WIDTH = LANES
GB_USED = 4 * DN_HEADS
assert GDN_CHUNK == DN_DK and GDN_GROUP * DN_DK == MXU_DIM and 3 * GB_USED <= GB_WIDTH
MASKED = -1e30


def _const_spec(shape):
    nd = len(shape)
    return pl.BlockSpec(shape, lambda *_: (0,) * nd, pipeline_mode=pl.Buffered(1))


def _params(n_axes):
    return pltpu.CompilerParams(dimension_semantics=("arbitrary",) * n_axes,
                                vmem_limit_bytes=VMEM_LIMIT_BYTES)


def _layer_norm_rows(r, g, b):
    mu = jnp.mean(r, axis=-1, keepdims=True)
    d = r - mu
    var = jnp.mean(d * d, axis=-1, keepdims=True)
    return d * lax.rsqrt(var + LN_EPS) * g + b


def _sigmoid(x):
    return 1.0 / (1.0 + jnp.exp(-x))


def _head_sums(sq, ones_ref):
    hi = sq.astype(BF16)
    lo = (sq - hi.astype(F32)).astype(BF16)
    ones = ones_ref[...]
    parts = []
    for c in range(0, sq.shape[1], MXU_DIM):
        parts.append(jnp.dot(hi[:, c:c + MXU_DIM], ones, preferred_element_type=F32)
                     + jnp.dot(lo[:, c:c + MXU_DIM], ones, preferred_element_type=F32))
    return jnp.concatenate(parts, axis=1)


def _ffn_kernel(x_ref, wg_ref, wu_ref, wd_ref, g_ref, b_ref, o_ref, h_scr):
    x = x_ref[...]
    xb = x.astype(BF16)
    for c in range(0, D_FF, FFN_COLS):
        gate = jnp.dot(xb, wg_ref[:, c:c + FFN_COLS], preferred_element_type=F32)
        up = jnp.dot(xb, wu_ref[:, c:c + FFN_COLS], preferred_element_type=F32)
        h_scr[:, c:c + FFN_COLS] = (gate * _sigmoid(gate) * up).astype(BF16)
    y = jnp.dot(h_scr[...], wd_ref[...], preferred_element_type=F32)
    o_ref[...] = _layer_norm_rows(ALPHA * x + 0.5 * y, g_ref[...], b_ref[...])


def _ffn_ln(x2d, wg, wu, wd, g, b):
    n = x2d.shape[0]
    rows = min(FFN_ROWS, n)
    row_spec = pl.BlockSpec((rows, D_MODEL), lambda i: (i, 0))
    return pl.pallas_call(
        _ffn_kernel,
        grid=(n // rows,),
        in_specs=[row_spec, _const_spec(wg.shape), _const_spec(wu.shape), _const_spec(wd.shape),
                  _const_spec(g.shape), _const_spec(b.shape)],
        out_specs=row_spec,
        out_shape=jax.ShapeDtypeStruct((n, D_MODEL), F32),
        scratch_shapes=[pltpu.VMEM((rows, D_FF), BF16)],
        compiler_params=_params(1),
        name="ffn_ln",
    )(x2d, wg, wu, wd, g, b)


def _proj_kernel(xp_ref, x_ref, xn_ref, watt_ref, wdn_ref, wz_ref, wab_ref, conv_ref,
                 nexpa_ref, dtb_ref, ones_ref,
                 qa_ref, ka_ref, va_ref, qn_ref, kn_ref, vn_ref, z_ref, gb_ref):
    t = pl.program_id(1)
    nt = pl.num_programs(1)
    rows = x_ref.shape[0]
    x = x_ref[...]
    xb = x.astype(BF16)

    att = jnp.dot(xb, watt_ref[...], preferred_element_type=F32)
    qa_ref[...] = (att[:, :ATT_Q] * HEAD_DIM ** -0.5).astype(BF16)
    ka_ref[...] = att[:, ATT_Q:ATT_Q + ATT_KV].astype(BF16)
    va_ref[...] = att[:, ATT_Q + ATT_KV:].astype(BF16)

    z_ref[...] = jnp.dot(xb, wz_ref[...], preferred_element_type=F32).astype(BF16)

    ab = jnp.dot(xb, wab_ref[...], preferred_element_type=F32)
    sp_in = ab + dtb_ref[...]
    softplus = jnp.maximum(sp_in, 0.0) + jnp.log1p(jnp.exp(-jnp.abs(sp_in)))
    lane = lax.broadcasted_iota(jnp.int32, ab.shape, 1)
    gb_ref[...] = jnp.where(lane < 2 * DN_HEADS, nexpa_ref[...] * softplus,
                            jnp.where(lane < 4 * DN_HEADS, _sigmoid(ab), 0.0))

    prev_ok = (t > 0).astype(F32)
    next_ok = (t < nt - 1).astype(F32)
    ext = jnp.concatenate([xp_ref[...] * prev_ok, x, xn_ref[...] * next_ok], axis=0).astype(BF16)
    p = jnp.dot(ext, wdn_ref[...], preferred_element_type=F32)
    n_ext = rows + 2 * SUBLANES
    conv = conv_ref[...]
    c = (pltpu.roll(p, 1, 0) * conv[0:1, :] + p * conv[1:2, :]
         + pltpu.roll(p, n_ext - 1, 0) * conv[2:3, :])
    c = c[SUBLANES:SUBLANES + rows, :]
    c = c * _sigmoid(c)
    q = c[:, :DN_K]
    k = c[:, DN_K:2 * DN_K]
    qn_ref[...] = (q * lax.rsqrt(_head_sums(q * q, ones_ref) + L2_EPS) * DN_DK ** -0.5).astype(BF16)
    kn_ref[...] = (k * lax.rsqrt(_head_sums(k * k, ones_ref) + L2_EPS)).astype(BF16)
    vn_ref[...] = c[:, 2 * DN_K:].astype(BF16)


def _proj(x3d, watt, wdn, wz, wab, conv, nexpa, dtb, ones):
    bsz, s, _ = x3d.shape
    rows = min(PROJ_ROWS, s)
    nblk = s // SUBLANES
    rb = rows // SUBLANES

    def tile(width):
        return pl.BlockSpec((None, rows, width), lambda b, t: (b, t, 0))

    prev_spec = pl.BlockSpec((None, SUBLANES, D_MODEL),
                             lambda b, t: (b, jnp.maximum(t * rb - 1, 0), 0))
    next_spec = pl.BlockSpec((None, SUBLANES, D_MODEL),
                             lambda b, t: (b, jnp.minimum((t + 1) * rb, nblk - 1), 0))
    widths = (ATT_Q, ATT_KV, ATT_KV, DN_K, DN_K, DN_V, DN_V)
    out_shape = [jax.ShapeDtypeStruct((bsz, s, w), BF16) for w in widths]
    out_shape.append(jax.ShapeDtypeStruct((bsz, s, GB_WIDTH), F32))
    out_specs = [tile(w) for w in widths] + [tile(GB_WIDTH)]
    consts = (watt, wdn, wz, wab, conv, nexpa, dtb, ones)
    return pl.pallas_call(
        _proj_kernel,
        grid=(bsz, s // rows),
        in_specs=[prev_spec, tile(D_MODEL), next_spec] + [_const_spec(c.shape) for c in consts],
        out_specs=out_specs,
        out_shape=out_shape,
        compiler_params=_params(2),
        name="in_proj",
    )(x3d, x3d, x3d, *consts)


def _attn_kernel(sink_ref, q_ref, kp_ref, k_ref, kn_ref, vp_ref, v_ref, vn_ref, bias_ref, o_ref):
    t = pl.program_id(1)
    nt = pl.num_programs(1)
    rows = q_ref.shape[0]
    nblk = rows // BLOCK
    kcat = jnp.concatenate([kp_ref[...], k_ref[...], kn_ref[...]], axis=0)
    vcat = jnp.concatenate([vp_ref[...], v_ref[...], vn_ref[...]], axis=0)
    col = lax.broadcasted_iota(jnp.int32, (BLOCK, 3 * BLOCK), 1)
    first_tile = t == 0
    last_tile = t == nt - 1
    for j in range(nblk):
        r0 = j * BLOCK
        edge = None
        if j == 0:
            edge = jnp.logical_and(first_tile, col < BLOCK)
        if j == nblk - 1:
            hi_edge = jnp.logical_and(last_tile, col >= 2 * BLOCK)
            edge = hi_edge if edge is None else jnp.logical_or(edge, hi_edge)
        for h in range(ATT_HEADS):
            kh = h // ATT_GROUP
            q = q_ref[r0:r0 + BLOCK, h * HEAD_DIM:(h + 1) * HEAD_DIM]
            kw = kcat[r0:r0 + 3 * BLOCK, kh * HEAD_DIM:(kh + 1) * HEAD_DIM]
            vw = vcat[r0:r0 + 3 * BLOCK, kh * HEAD_DIM:(kh + 1) * HEAD_DIM]
            s = lax.dot_general(q, kw, (((1,), (1,)), ((), ())), preferred_element_type=F32)
            s = s + bias_ref[h]
            if edge is not None:
                s = jnp.where(edge, MASKED, s)
            sink = sink_ref[h]
            m = jnp.maximum(jnp.max(s, axis=-1, keepdims=True), sink)
            p = jnp.exp(s - m)
            denom = jnp.sum(p, axis=-1, keepdims=True) + jnp.exp(sink - m)
            o = jnp.dot(p.astype(BF16), vw, preferred_element_type=F32) / denom
            o_ref[r0:r0 + BLOCK, h * HEAD_DIM:(h + 1) * HEAD_DIM] = o.astype(BF16)


def _attention(qa, ka, va, bias, sink):
    bsz, s, _ = qa.shape
    rows = min(ATT_ROWS, s)
    rb = rows // BLOCK
    nblk = s // BLOCK

    def tile(width):
        return pl.BlockSpec((None, rows, width), lambda b, t: (b, t, 0))

    prev_spec = pl.BlockSpec((None, BLOCK, ATT_KV), lambda b, t: (b, jnp.maximum(t * rb - 1, 0), 0))
    next_spec = pl.BlockSpec((None, BLOCK, ATT_KV),
                             lambda b, t: (b, jnp.minimum((t + 1) * rb, nblk - 1), 0))
    return pl.pallas_call(
        _attn_kernel,
        grid=(bsz, s // rows),
        in_specs=[pl.BlockSpec(memory_space=pltpu.SMEM),
                  tile(ATT_Q), prev_spec, tile(ATT_KV), next_spec,
                  prev_spec, tile(ATT_KV), next_spec, _const_spec(bias.shape)],
        out_specs=tile(ATT_Q),
        out_shape=jax.ShapeDtypeStruct((bsz, s, ATT_Q), BF16),
        compiler_params=_params(2),
        name="window_attn",
    )(sink, qa, ka, ka, ka, va, va, va, bias)


def _segmented_cumsum(x, axis, seg, reverse):
    n = x.shape[axis]
    pos = lax.broadcasted_iota(jnp.int32, x.shape, axis) % seg
    step = 1
    while step < seg:
        if reverse:
            shifted = pltpu.roll(x, n - step, axis)
            keep = pos < seg - step
        else:
            shifted = pltpu.roll(x, step, axis)
            keep = pos >= step
        x = x + jnp.where(keep, shifted, 0.0)
        step *= 2
    return x


def _block_diag(xb, nblk):
    head = lax.broadcasted_iota(jnp.int32, xb.shape, 1) // DN_DK
    zero = jnp.zeros_like(xb)
    return jnp.concatenate([jnp.where(head == g, xb, zero) for g in range(nblk)], axis=0)


def _split3_packed(x):
    t1 = x.astype(BF16).astype(F32)
    r1 = x - t1
    t2 = r1.astype(BF16).astype(F32)
    t3 = (r1 - t2).astype(BF16).astype(F32)
    return (t1 + pltpu.roll(t2, GB_USED, 1) + pltpu.roll(t3, 2 * GB_USED, 1)).astype(BF16)


def _gdn_kernel(q_ref, k_ref, v_ref, gb_ref, grow_ref, expand_ref, o_ref, s_scr, *, reverse):
    t = pl.program_id(1)
    rows = q_ref.shape[0]
    chunk = GDN_CHUNK
    nchunk = rows // chunk
    gw = GDN_GROUP * DN_DK
    pw = GDN_PAIR * DN_DK
    ngroup = DN_HEADS // GDN_GROUP
    npair = GDN_GROUP // GDN_PAIR

    @pl.when(t == 0)
    def _():
        s_scr[...] = jnp.zeros_like(s_scr)

    gb = gb_ref[...]
    lane = lax.broadcasted_iota(jnp.int32, gb.shape, 1)
    scal = jnp.where(lane < 2 * DN_HEADS, _segmented_cumsum(gb, 0, chunk, reverse), gb)
    spread = jnp.dot(_split3_packed(scal), expand_ref[...], preferred_element_type=F32)
    gx = spread[:, :DN_K]
    bx = spread[:, DN_K:]
    egx = jnp.exp(gx)

    ri = lax.broadcasted_iota(jnp.int32, (chunk, gw), 0)
    cj = lax.broadcasted_iota(jnp.int32, (chunk, gw), 1) % chunk
    incl = (ri <= cj) if reverse else (ri >= cj)
    strict = (ri < cj) if reverse else (ri > cj)
    eye = (ri == cj).astype(F32)
    pair_diag = (lax.broadcasted_iota(jnp.int32, (pw, pw), 0) // DN_DK
                 == lax.broadcasted_iota(jnp.int32, (pw, pw), 1) // DN_DK)
    last = 0 if reverse else chunk - 1
    order = list(range(nchunk - 1, -1, -1)) if reverse else list(range(nchunk))
    insts = [(c, g) for c in order for g in range(ngroup)]
    nt_dims = (((1,), (1,)), ((), ()))
    tn_dims = (((0,), (0,)), ((), ()))
    grow_base = ngroup if reverse else 0

    def rsl(c):
        return slice(c * chunk, (c + 1) * chunk)

    def gsl(g):
        return slice(g * gw, (g + 1) * gw)

    st = {key: {} for key in insts}

    for c, g in insts:
        d = st[c, g]
        kb = k_ref[rsl(c), gsl(g)]
        qb = q_ref[rsl(c), gsl(g)]
        kq = lax.dot_general(jnp.concatenate([kb, qb], axis=0), _block_diag(kb, GDN_GROUP), nt_dims,
                             preferred_element_type=F32)
        d["kf"] = kb.astype(F32)
        d["qf"] = qb.astype(F32)
        d["kk"] = kq[:chunk]
        d["qk"] = kq[chunk:]

    for c, g in insts:
        d = st[c, g]
        gx4 = gx[rsl(c), gsl(g)]
        grow = _segmented_cumsum(grow_ref[c], 1, chunk, reverse)[grow_base + g:grow_base + g + 1, :]
        decay = jnp.where(incl, jnp.exp(jnp.where(incl, gx4 - grow, 0.0)), 0.0)
        d["gx"] = gx4
        d["decay"] = decay
        d["neg_a"] = jnp.where(strict, -(bx[rsl(c), gsl(g)] * d.pop("kk") * decay), 0.0)

    for c, g in insts:
        d = st[c, g]
        d["inv"] = eye + d["neg_a"]
        pb = d.pop("neg_a").astype(BF16)
        d["power"] = jnp.dot(pb, _block_diag(pb, GDN_GROUP), preferred_element_type=F32)
    step = 4
    while step < chunk:
        for c, g in insts:
            d = st[c, g]
            pb = d["power"].astype(BF16)
            both = jnp.dot(jnp.concatenate([d["inv"].astype(BF16), pb], axis=0),
                           _block_diag(pb, GDN_GROUP), preferred_element_type=F32)
            d["inv"] = d["inv"] + both[:chunk]
            d["power"] = both[chunk:]
        step *= 2
    for c, g in insts:
        d = st[c, g]
        pb = d.pop("power").astype(BF16)
        d["inv"] = d["inv"] + jnp.dot(d["inv"].astype(BF16), _block_diag(pb, GDN_GROUP),
                                      preferred_element_type=F32)

    for c, g in insts:
        d = st[c, g]
        b4 = bx[rsl(c), gsl(g)]
        eg4 = egx[rsl(c), gsl(g)]
        yk = ((b4 * eg4) * d["kf"]).astype(BF16)
        yv = (b4 * v_ref[rsl(c), gsl(g)].astype(F32)).astype(BF16)
        rhs = jnp.concatenate([_block_diag(yk, GDN_GROUP), _block_diag(yv, GDN_GROUP)], axis=1)
        wu = jnp.dot(d.pop("inv").astype(BF16), rhs, preferred_element_type=F32).astype(BF16)
        d["eg"] = eg4
        d["w"] = wu[:, :gw]
        d["u"] = wu[:, gw:]

    for c, g in insts:
        d = st[c, g]
        att = (d.pop("qk") * d.pop("decay")).astype(BF16)
        rhs = jnp.concatenate([_block_diag(d["w"], GDN_GROUP), _block_diag(d["u"], GDN_GROUP)], axis=1)
        aw = jnp.dot(att, rhs, preferred_element_type=F32)
        d["q_eff"] = (d.pop("qf") * d.pop("eg") - aw[:, :gw]).astype(BF16)
        d["o0"] = aw[:, gw:]

    for c, g in insts:
        d = st[c, g]
        g_end = d["gx"][last:last + 1, :]
        kd = (d.pop("kf") * jnp.exp(g_end - d.pop("gx"))).astype(BF16)
        d["gl"] = jnp.exp(g_end)
        d["m"] = []
        d["b"] = []
        for p in range(npair):
            psl = slice(p * pw, (p + 1) * pw)
            z = lax.dot_general(kd[:, psl], jnp.concatenate([d["w"][:, psl], d["u"][:, psl]], axis=1),
                                tn_dims, preferred_element_type=F32)
            d["m"].append(jnp.where(pair_diag, z[:, :pw], 0.0).astype(BF16))
            d["b"].append(jnp.where(pair_diag, z[:, pw:], 0.0))

    states = [s_scr[i] for i in range(ngroup * npair)]
    for c in order:
        for g in range(ngroup):
            d = st[c, g]
            for p in range(npair):
                i = g * npair + p
                psl = slice(p * pw, (p + 1) * pw)
                state = states[i]
                res = jnp.dot(jnp.concatenate([d["q_eff"][:, psl], d["m"][p]], axis=0),
                              state.astype(BF16), preferred_element_type=F32)
                o_ref[rsl(c), g * gw + p * pw:g * gw + (p + 1) * pw] = res[:chunk] + d["o0"][:, psl]
                states[i] = d["gl"][:, psl] * state - res[chunk:] + d["b"][p]
    for i, state in enumerate(states):
        s_scr[i] = state


def _gdn_expand_matrix(reverse):
    dcol = DN_HEADS if reverse else 0
    bcol = 2 * DN_HEADS + dcol
    e = np.zeros((GB_WIDTH, 2 * DN_K), np.float32)
    for term in range(3):
        for h in range(DN_HEADS):
            e[term * GB_USED + dcol + h, h * DN_DK:(h + 1) * DN_DK] = 1.0
            e[term * GB_USED + bcol + h, DN_K + h * DN_DK:DN_K + (h + 1) * DN_DK] = 1.0
    return jnp.asarray(e, dtype=BF16)


def _gdn_scan(qn, kn, vn, gb, grow, reverse):
    bsz, s, _ = qn.shape
    rows = min(GDN_ROWS, s)
    nt = s // rows
    nchunk = rows // GDN_CHUNK
    expand = _gdn_expand_matrix(reverse)

    def tmap(t):
        return nt - 1 - t if reverse else t

    def tile(width):
        return pl.BlockSpec((None, rows, width), lambda b, t: (b, tmap(t), 0))

    grow_spec = pl.BlockSpec((None, nchunk) + grow.shape[2:], lambda b, t: (b, tmap(t), 0, 0))
    return pl.pallas_call(
        functools.partial(_gdn_kernel, reverse=reverse),
        grid=(bsz, nt),
        in_specs=[tile(DN_K), tile(DN_K), tile(DN_V), tile(GB_WIDTH), grow_spec,
                  _const_spec(expand.shape)],
        out_specs=tile(DN_V),
        out_shape=jax.ShapeDtypeStruct((bsz, s, DN_V), F32),
        scratch_shapes=[pltpu.VMEM((DN_HEADS // GDN_PAIR, GDN_PAIR * DN_DK, GDN_PAIR * DN_DV), F32)],
        compiler_params=_params(2),
        name="gdn_bwd" if reverse else "gdn_fwd",
    )(qn, kn, vn, gb, grow, expand)


def _decay_rows(gb):
    bsz, s, _ = gb.shape
    ngroup = DN_HEADS // GDN_GROUP
    g = gb[:, :, :2 * DN_HEADS].reshape(bsz, s // GDN_CHUNK, GDN_CHUNK, 2, ngroup, GDN_GROUP)
    g = jnp.transpose(g, (0, 1, 3, 4, 5, 2)).reshape(bsz, s // GDN_CHUNK, 2 * ngroup,
                                                     GDN_GROUP * GDN_CHUNK)
    return jnp.pad(g, ((0, 0), (0, 0), (0, SUBLANES - 2 * ngroup), (0, 0)))


def _mix_kernel(x_ref, ao_ref, of_ref, ob_ref, z_ref, nw_ref, watt_ref, wdn_ref, wgate_ref,
                bgate_ref, wo_ref, g_ref, b_ref, ones_ref, o_ref, mix_scr):
    x = x_ref[...]
    xb = x.astype(BF16)
    o = of_ref[...] + ob_ref[...]
    ms = _head_sums(o * o, ones_ref) * (1.0 / DN_DV)
    z = z_ref[...].astype(F32)
    dn = (o * lax.rsqrt(ms + RMS_EPS) * nw_ref[...] * (z * _sigmoid(z))).astype(BF16)
    ao = ao_ref[...]
    for c in range(0, D_MODEL, MIX_COLS):
        cols = slice(c, c + MIX_COLS)
        gcols = slice(D_MODEL + c, D_MODEL + c + MIX_COLS)
        y_att = jnp.dot(ao, watt_ref[:, cols], preferred_element_type=F32)
        y_dn = jnp.dot(dn, wdn_ref[:, cols], preferred_element_type=F32)
        g_att = _sigmoid(jnp.dot(xb, wgate_ref[:, cols], preferred_element_type=F32) + bgate_ref[:, cols])
        g_dn = _sigmoid(jnp.dot(xb, wgate_ref[:, gcols], preferred_element_type=F32) + bgate_ref[:, gcols])
        mix_scr[:, cols] = (g_att * y_att + g_dn * y_dn).astype(BF16)
    m = jnp.dot(mix_scr[...], wo_ref[...], preferred_element_type=F32)
    o_ref[...] = _layer_norm_rows(ALPHA * x + m, g_ref[...], b_ref[...])


def _mix(x2d, ao, of, ob, z, nw, watt, wdn, wgate, bgate, wo, g, b, ones):
    n = x2d.shape[0]
    rows = min(MIX_ROWS, n)

    def tile(width):
        return pl.BlockSpec((rows, width), lambda i: (i, 0))

    consts = (nw, watt, wdn, wgate, bgate, wo, g, b, ones)
    return pl.pallas_call(
        _mix_kernel,
        grid=(n // rows,),
        in_specs=[tile(D_MODEL), tile(ATT_Q), tile(DN_V), tile(DN_V), tile(DN_V)]
                 + [_const_spec(c.shape) for c in consts],
        out_specs=tile(D_MODEL),
        out_shape=jax.ShapeDtypeStruct((n, D_MODEL), F32),
        scratch_shapes=[pltpu.VMEM((rows, D_MODEL), BF16)],
        compiler_params=_params(1),
        name="mix_out",
    )(x2d, ao, of, ob, z, *consts)


def _t5_buckets(rel):
    half = REL_BUCKETS // 2
    ret = (rel > 0).astype(np.int32) * half
    n = np.abs(rel)
    max_exact = half // 2
    large = max_exact + (np.log(np.maximum(n, 1) / max_exact) / np.log(REL_MAX_DIST / max_exact)
                         * (half - max_exact)).astype(np.int32)
    large = np.minimum(large, half - 1)
    return (ret + np.where(n < max_exact, n, large)).astype(np.int32)


def _band_bias(rel_bias):
    i = np.arange(BLOCK)[:, None]
    j = np.arange(3 * BLOCK)[None, :]
    rel = j - BLOCK - i
    bias = jnp.transpose(rel_bias.astype(F32)[_t5_buckets(rel)], (2, 0, 1))
    return jnp.where(jnp.asarray(np.abs(rel) <= WINDOW)[None], bias, MASKED)


def _head_block_ones():
    idx = np.arange(MXU_DIM) // DN_DK
    return jnp.asarray(idx[:, None] == idx[None, :], dtype=BF16)


def _row(v):
    return v.reshape(1, -1).astype(F32)


def _pad_lanes(v, width):
    v = v.reshape(1, -1).astype(F32)
    return jnp.pad(v, ((0, 0), (0, width - v.shape[1])))


def _trunk(x, layers, bias, ones):
    bsz, s, _ = x.shape
    x2d = x.reshape(bsz * s, D_MODEL)
    for lp in layers:
        x2d = _ffn_ln(x2d, *lp["ffn0"])
        x3d = x2d.reshape(bsz, s, D_MODEL)
        qa, ka, va, qn, kn, vn, z, gb = _proj(x3d, *lp["proj"], ones)
        ao = _attention(qa, ka, va, bias, lp["sink"])
        grow = _decay_rows(gb)
        o_fwd = _gdn_scan(qn, kn, vn, gb, grow, reverse=False)
        o_bwd = _gdn_scan(qn, kn, vn, gb, grow, reverse=True)
        flat = lambda a: a.reshape(bsz * s, a.shape[-1])
        x2d = _mix(x2d, flat(ao), flat(o_fwd), flat(o_bwd), flat(z), *lp["mix"], ones)
        x2d = _ffn_ln(x2d, *lp["ffn1"])
    return x2d.reshape(bsz, s, D_MODEL)


def kernel(x_prompt, x_sample, ln_g, ln_b, ffn_w_gate, ffn_w_up, ffn_w_down, w_in, attn_sink, rel_bias,
           dn_conv, dn_a_log, dn_dt_bias, dn_norm_w, w_att_out, w_dn_out, w_gate, b_gate, w_o):
    bias = _band_bias(rel_bias)
    ones = _head_block_ones()
    c_att = ATT_Q + 2 * ATT_KV
    c_dn = c_att + DN_QKV
    c_z = c_dn + DN_V
    layers = []
    for l in range(DEPTH):
        wi = w_in[l]
        wab = jnp.pad(wi[:, c_z:], ((0, 0), (0, GB_WIDTH - 4 * DN_HEADS))).astype(BF16)
        nexpa = _pad_lanes(-jnp.exp(dn_a_log[l].astype(F32)), GB_WIDTH)
        dtb = _pad_lanes(dn_dt_bias[l], GB_WIDTH)
        layers.append({
            "ffn0": (ffn_w_gate[l, 0].astype(BF16), ffn_w_up[l, 0].astype(BF16),
                     ffn_w_down[l, 0].astype(BF16), _row(ln_g[l, 0]), _row(ln_b[l, 0])),
            "ffn1": (ffn_w_gate[l, 1].astype(BF16), ffn_w_up[l, 1].astype(BF16),
                     ffn_w_down[l, 1].astype(BF16), _row(ln_g[l, 2]), _row(ln_b[l, 2])),
            "proj": (wi[:, :c_att].astype(BF16), wi[:, c_att:c_dn].astype(BF16),
                     wi[:, c_dn:c_z].astype(BF16), wab, dn_conv[l].astype(F32), nexpa, dtb),
            "sink": attn_sink[l].astype(F32),
            "mix": (_row(jnp.tile(dn_norm_w[l], DN_HEADS)), w_att_out[l].astype(BF16),
                    w_dn_out[l].astype(BF16), w_gate[l].astype(BF16), _row(b_gate[l]),
                    w_o[l].astype(BF16), _row(ln_g[l, 1]), _row(ln_b[l, 1])),
        })
    y_prompt = _trunk(x_prompt, layers, bias, ones)
    y_sample = _trunk(x_sample, layers, bias, ones)
    return (y_prompt, y_sample)
```

```python
import functools
import math

import numpy as np
import jax
import jax.numpy as jnp
from jax import lax
from jax.experimental import pallas as pl
from jax.experimental.pallas import tpu as pltpu

F32 = jnp.float32
BF16 = jnp.bfloat16

D_MODEL = 1024
DEPTH = 2
ATT_HEADS = 8
ATT_KV_HEADS = 2
ATT_GROUP = ATT_HEADS // ATT_KV_HEADS
HEAD_DIM = 64
WINDOW = 128
BLOCK = 128
REL_BUCKETS = 32
REL_MAX_DIST = 128
DN_HEADS = 8
DN_DK = 64
DN_DV = 64
DN_CONV = 3
D_FF = 2816
ALPHA = (2 * DEPTH) ** 0.25
LN_EPS = 1e-5
RMS_EPS = 1e-6
L2_EPS = 1e-6

ATT_Q = ATT_HEADS * HEAD_DIM
ATT_KV = ATT_KV_HEADS * HEAD_DIM
DN_K = DN_HEADS * DN_DK
DN_V = DN_HEADS * DN_DV
DN_QKV = 2 * DN_K + DN_V

LANES = 128
SUBLANES = 8
MXU_DIM = 256
VMEM_LIMIT_BYTES = 56 * 1024 * 1024

FFN_ROWS = 512
FFN_COLS = 256
PROJ_ROWS = 512
ATT_ROWS = 512
GDN_ROWS = 256
GDN_CHUNK = 64
MIX_ROWS = 512
MIX_COLS = 256
GDN_GROUP = 4
GDN_PAIR = 2
GB_WIDTH = LANES
GB_USED = 4 * DN_HEADS
assert GDN_CHUNK == DN_DK and GDN_GROUP * DN_DK == MXU_DIM and 3 * GB_USED <= GB_WIDTH
MASKED = -1e30


def _const_spec(shape):
    nd = len(shape)
    return pl.BlockSpec(shape, lambda *_: (0,) * nd, pipeline_mode=pl.Buffered(1))


def _params(n_axes):
    return pltpu.CompilerParams(dimension_semantics=("arbitrary",) * n_axes,
                                vmem_limit_bytes=VMEM_LIMIT_BYTES)


def _layer_norm_rows(r, g, b):
    mu = jnp.mean(r, axis=-1, keepdims=True)
    d = r - mu
    var = jnp.mean(d * d, axis=-1, keepdims=True)
    return d * lax.rsqrt(var + LN_EPS) * g + b


def _sigmoid(x):
    return 1.0 / (1.0 + jnp.exp(-x))


def _head_sums(sq, ones_ref):
    hi = sq.astype(BF16)
    lo = (sq - hi.astype(F32)).astype(BF16)
    ones = ones_ref[...]
    parts = []
    for c in range(0, sq.shape[1], MXU_DIM):
        parts.append(jnp.dot(hi[:, c:c + MXU_DIM], ones, preferred_element_type=F32)
                     + jnp.dot(lo[:, c:c + MXU_DIM], ones, preferred_element_type=F32))
    return jnp.concatenate(parts, axis=1)


def _ffn_kernel(x_ref, wg_ref, wu_ref, wd_ref, g_ref, b_ref, o_ref, h_scr):
    x = x_ref[...]
    xb = x.astype(BF16)
    for c in range(0, D_FF, FFN_COLS):
        gate = jnp.dot(xb, wg_ref[:, c:c + FFN_COLS], preferred_element_type=F32)
        up = jnp.dot(xb, wu_ref[:, c:c + FFN_COLS], preferred_element_type=F32)
        h_scr[:, c:c + FFN_COLS] = (gate * _sigmoid(gate) * up).astype(BF16)
    y = jnp.dot(h_scr[...], wd_ref[...], preferred_element_type=F32)
    o_ref[...] = _layer_norm_rows(ALPHA * x + 0.5 * y, g_ref[...], b_ref[...])


def _ffn_ln(x2d, wg, wu, wd, g, b):
    n = x2d.shape[0]
    rows = min(FFN_ROWS, n)
    row_spec = pl.BlockSpec((rows, D_MODEL), lambda i: (i, 0))
    return pl.pallas_call(
        _ffn_kernel,
        grid=(n // rows,),
        in_specs=[row_spec, _const_spec(wg.shape), _const_spec(wu.shape), _const_spec(wd.shape),
                  _const_spec(g.shape), _const_spec(b.shape)],
        out_specs=row_spec,
        out_shape=jax.ShapeDtypeStruct((n, D_MODEL), F32),
        scratch_shapes=[pltpu.VMEM((rows, D_FF), BF16)],
        compiler_params=_params(1),
        name="ffn_ln",
    )(x2d, wg, wu, wd, g, b)


def _proj_kernel(xp_ref, x_ref, xn_ref, watt_ref, wdn_ref, wz_ref, wab_ref, conv_ref,
                 nexpa_ref, dtb_ref, ones_ref,
                 qa_ref, ka_ref, va_ref, qn_ref, kn_ref, vn_ref, z_ref, gb_ref):
    t = pl.program_id(1)
    nt = pl.num_programs(1)
    rows = x_ref.shape[0]
    x = x_ref[...]
    xb = x.astype(BF16)

    att = jnp.dot(xb, watt_ref[...], preferred_element_type=F32)
    qa_ref[...] = (att[:, :ATT_Q] * HEAD_DIM ** -0.5).astype(BF16)
    ka_ref[...] = att[:, ATT_Q:ATT_Q + ATT_KV].astype(BF16)
    va_ref[...] = att[:, ATT_Q + ATT_KV:].astype(BF16)

    z_ref[...] = jnp.dot(xb, wz_ref[...], preferred_element_type=F32).astype(BF16)

    ab = jnp.dot(xb, wab_ref[...], preferred_element_type=F32)
    sp_in = ab + dtb_ref[...]
    softplus = jnp.maximum(sp_in, 0.0) + jnp.log1p(jnp.exp(-jnp.abs(sp_in)))
    lane = lax.broadcasted_iota(jnp.int32, ab.shape, 1)
    gb_ref[...] = jnp.where(lane < 2 * DN_HEADS, nexpa_ref[...] * softplus,
                            jnp.where(lane < 4 * DN_HEADS, _sigmoid(ab), 0.0))

    prev_ok = (t > 0).astype(F32)
    next_ok = (t < nt - 1).astype(F32)
    ext = jnp.concatenate([xp_ref[...] * prev_ok, x, xn_ref[...] * next_ok], axis=0).astype(BF16)
    p = jnp.dot(ext, wdn_ref[...], preferred_element_type=F32)
    n_ext = rows + 2 * SUBLANES
    conv = conv_ref[...]
    c = (pltpu.roll(p, 1, 0) * conv[0:1, :] + p * conv[1:2, :]
         + pltpu.roll(p, n_ext - 1, 0) * conv[2:3, :])
    c = c[SUBLANES:SUBLANES + rows, :]
    c = c * _sigmoid(c)
    q = c[:, :DN_K]
    k = c[:, DN_K:2 * DN_K]
    qn_ref[...] = (q * lax.rsqrt(_head_sums(q * q, ones_ref) + L2_EPS) * DN_DK ** -0.5).astype(BF16)
    kn_ref[...] = (k * lax.rsqrt(_head_sums(k * k, ones_ref) + L2_EPS)).astype(BF16)
    vn_ref[...] = c[:, 2 * DN_K:].astype(BF16)


def _proj(x3d, watt, wdn, wz, wab, conv, nexpa, dtb, ones):
    bsz, s, _ = x3d.shape
    rows = min(PROJ_ROWS, s)
    nblk = s // SUBLANES
    rb = rows // SUBLANES

    def tile(width):
        return pl.BlockSpec((None, rows, width), lambda b, t: (b, t, 0))

    prev_spec = pl.BlockSpec((None, SUBLANES, D_MODEL),
                             lambda b, t: (b, jnp.maximum(t * rb - 1, 0), 0))
    next_spec = pl.BlockSpec((None, SUBLANES, D_MODEL),
                             lambda b, t: (b, jnp.minimum((t + 1) * rb, nblk - 1), 0))
    widths = (ATT_Q, ATT_KV, ATT_KV, DN_K, DN_K, DN_V, DN_V)
    out_shape = [jax.ShapeDtypeStruct((bsz, s, w), BF16) for w in widths]
    out_shape.append(jax.ShapeDtypeStruct((bsz, s, GB_WIDTH), F32))
    out_specs = [tile(w) for w in widths] + [tile(GB_WIDTH)]
    consts = (watt, wdn, wz, wab, conv, nexpa, dtb, ones)
    return pl.pallas_call(
        _proj_kernel,
        grid=(bsz, s // rows),
        in_specs=[prev_spec, tile(D_MODEL), next_spec] + [_const_spec(c.shape) for c in consts],
        out_specs=out_specs,
        out_shape=out_shape,
        compiler_params=_params(2),
        name="in_proj",
    )(x3d, x3d, x3d, *consts)


def _attn_kernel(sink_ref, q_ref, kp_ref, k_ref, kn_ref, vp_ref, v_ref, vn_ref, bias_ref, o_ref):
    t = pl.program_id(1)
    nt = pl.num_programs(1)
    rows = q_ref.shape[0]
    nblk = rows // BLOCK
    kcat = jnp.concatenate([kp_ref[...], k_ref[...], kn_ref[...]], axis=0)
    vcat = jnp.concatenate([vp_ref[...], v_ref[...], vn_ref[...]], axis=0)
    col = lax.broadcasted_iota(jnp.int32, (BLOCK, 3 * BLOCK), 1)
    first_tile = t == 0
    last_tile = t == nt - 1
    for j in range(nblk):
        r0 = j * BLOCK
        edge = None
        if j == 0:
            edge = jnp.logical_and(first_tile, col < BLOCK)
        if j == nblk - 1:
            hi_edge = jnp.logical_and(last_tile, col >= 2 * BLOCK)
            edge = hi_edge if edge is None else jnp.logical_or(edge, hi_edge)
        heads = range(ATT_HEADS)
        kws = [kcat[r0:r0 + 3 * BLOCK, kh * HEAD_DIM:(kh + 1) * HEAD_DIM] for kh in range(ATT_KV_HEADS)]
        vws = [vcat[r0:r0 + 3 * BLOCK, kh * HEAD_DIM:(kh + 1) * HEAD_DIM] for kh in range(ATT_KV_HEADS)]
        logits = []
        for h in heads:
            q = q_ref[r0:r0 + BLOCK, h * HEAD_DIM:(h + 1) * HEAD_DIM]
            s = lax.dot_general(q, kws[h // ATT_GROUP], (((1,), (1,)), ((), ())),
                                preferred_element_type=F32) + bias_ref[h]
            logits.append(s if edge is None else jnp.where(edge, MASKED, s))
        tops = [jnp.maximum(jnp.max(logits[h], axis=-1, keepdims=True), sink_ref[h]) for h in heads]
        probs = [jnp.exp(logits[h] - tops[h]) for h in heads]
        denoms = [jnp.sum(probs[h], axis=-1, keepdims=True) + jnp.exp(sink_ref[h] - tops[h])
                  for h in heads]
        outs = [jnp.dot(probs[h].astype(BF16), vws[h // ATT_GROUP], preferred_element_type=F32)
                / denoms[h] for h in heads]
        heads_per_store = LANES // HEAD_DIM
        for h in range(0, ATT_HEADS, heads_per_store):
            o_ref[r0:r0 + BLOCK, h * HEAD_DIM:(h + heads_per_store) * HEAD_DIM] = jnp.concatenate(
                outs[h:h + heads_per_store], axis=1).astype(BF16)


def _attention(qa, ka, va, bias, sink):
    bsz, s, _ = qa.shape
    rows = min(ATT_ROWS, s)
    rb = rows // BLOCK
    nblk = s // BLOCK

    def tile(width):
        return pl.BlockSpec((None, rows, width), lambda b, t: (b, t, 0))

    prev_spec = pl.BlockSpec((None, BLOCK, ATT_KV), lambda b, t: (b, jnp.maximum(t * rb - 1, 0), 0))
    next_spec = pl.BlockSpec((None, BLOCK, ATT_KV),
                             lambda b, t: (b, jnp.minimum((t + 1) * rb, nblk - 1), 0))
    return pl.pallas_call(
        _attn_kernel,
        grid=(bsz, s // rows),
        in_specs=[pl.BlockSpec(memory_space=pltpu.SMEM),
                  tile(ATT_Q), prev_spec, tile(ATT_KV), next_spec,
                  prev_spec, tile(ATT_KV), next_spec, _const_spec(bias.shape)],
        out_specs=tile(ATT_Q),
        out_shape=jax.ShapeDtypeStruct((bsz, s, ATT_Q), BF16),
        compiler_params=_params(2),
        name="window_attn",
    )(sink, qa, ka, ka, ka, va, va, va, bias)


def _segmented_cumsum(x, axis, seg, reverse):
    n = x.shape[axis]
    pos = lax.broadcasted_iota(jnp.int32, x.shape, axis) % seg
    step = 1
    while step < seg:
        if reverse:
            shifted = pltpu.roll(x, n - step, axis)
            keep = pos < seg - step
        else:
            shifted = pltpu.roll(x, step, axis)
            keep = pos >= step
        x = x + jnp.where(keep, shifted, 0.0)
        step *= 2
    return x


def _block_diag(xb, nblk):
    head = lax.broadcasted_iota(jnp.int32, xb.shape, 1) // DN_DK
    zero = jnp.zeros_like(xb)
    return jnp.concatenate([jnp.where(head == g, xb, zero) for g in range(nblk)], axis=0)


def _split3_packed(x):
    t1 = x.astype(BF16).astype(F32)
    r1 = x - t1
    t2 = r1.astype(BF16).astype(F32)
    t3 = (r1 - t2).astype(BF16).astype(F32)
    return (t1 + pltpu.roll(t2, GB_USED, 1) + pltpu.roll(t3, 2 * GB_USED, 1)).astype(BF16)


def _gdn_kernel(q_ref, k_ref, v_ref, gb_ref, grow_ref, expand_ref, o_ref, s_scr, *, reverse):
    t = pl.program_id(1)
    rows = q_ref.shape[0]
    chunk = GDN_CHUNK
    nchunk = rows // chunk
    gw = GDN_GROUP * DN_DK
    pw = GDN_PAIR * DN_DK
    ngroup = DN_HEADS // GDN_GROUP
    npair = GDN_GROUP // GDN_PAIR

    @pl.when(t == 0)
    def _():
        s_scr[...] = jnp.zeros_like(s_scr)

    gb = gb_ref[...]
    lane = lax.broadcasted_iota(jnp.int32, gb.shape, 1)
    scal = jnp.where(lane < 2 * DN_HEADS, _segmented_cumsum(gb, 0, chunk, reverse), gb)
    spread = jnp.dot(_split3_packed(scal), expand_ref[...], preferred_element_type=F32)
    gx = spread[:, :DN_K]
    bx = spread[:, DN_K:]
    egx = jnp.exp(gx)

    ri = lax.broadcasted_iota(jnp.int32, (chunk, gw), 0)
    cj = lax.broadcasted_iota(jnp.int32, (chunk, gw), 1) % chunk
    incl = (ri <= cj) if reverse else (ri >= cj)
    strict = (ri < cj) if reverse else (ri > cj)
    eye = (ri == cj).astype(F32)
    pair_diag = (lax.broadcasted_iota(jnp.int32, (pw, pw), 0) // DN_DK
                 == lax.broadcasted_iota(jnp.int32, (pw, pw), 1) // DN_DK)
    last = 0 if reverse else chunk - 1
    order = list(range(nchunk - 1, -1, -1)) if reverse else list(range(nchunk))
    insts = [(c, g) for c in order for g in range(ngroup)]
    nt_dims = (((1,), (1,)), ((), ()))
    tn_dims = (((0,), (0,)), ((), ()))
    grow_base = ngroup if reverse else 0

    def rsl(c):
        return slice(c * chunk, (c + 1) * chunk)

    def gsl(g):
        return slice(g * gw, (g + 1) * gw)

    st = {key: {} for key in insts}

    for c, g in insts:
        d = st[c, g]
        kb = k_ref[rsl(c), gsl(g)]
        qb = q_ref[rsl(c), gsl(g)]
        kq = lax.dot_general(jnp.concatenate([kb, qb], axis=0), _block_diag(kb, GDN_GROUP), nt_dims,
                             preferred_element_type=F32)
        d["kf"] = kb.astype(F32)
        d["qf"] = qb.astype(F32)
        d["kk"] = kq[:chunk]
        d["qk"] = kq[chunk:]

    for c, g in insts:
        d = st[c, g]
        gx4 = gx[rsl(c), gsl(g)]
        grow = _segmented_cumsum(grow_ref[c], 1, chunk, reverse)[grow_base + g:grow_base + g + 1, :]
        decay = jnp.where(incl, jnp.exp(jnp.where(incl, gx4 - grow, 0.0)), 0.0)
        d["gx"] = gx4
        d["decay"] = decay
        d["neg_a"] = jnp.where(strict, -(bx[rsl(c), gsl(g)] * d.pop("kk") * decay), 0.0)

    for c, g in insts:
        d = st[c, g]
        d["inv"] = eye + d["neg_a"]
        pb = d.pop("neg_a").astype(BF16)
        d["power"] = jnp.dot(pb, _block_diag(pb, GDN_GROUP), preferred_element_type=F32)
    step = 4
    while step < chunk:
        for c, g in insts:
            d = st[c, g]
            pb = d["power"].astype(BF16)
            both = jnp.dot(jnp.concatenate([d["inv"].astype(BF16), pb], axis=0),
                           _block_diag(pb, GDN_GROUP), preferred_element_type=F32)
            d["inv"] = d["inv"] + both[:chunk]
            d["power"] = both[chunk:]
        step *= 2
    for c, g in insts:
        d = st[c, g]
        pb = d.pop("power").astype(BF16)
        d["inv"] = d["inv"] + jnp.dot(d["inv"].astype(BF16), _block_diag(pb, GDN_GROUP),
                                      preferred_element_type=F32)

    for c, g in insts:
        d = st[c, g]
        b4 = bx[rsl(c), gsl(g)]
        eg4 = egx[rsl(c), gsl(g)]
        yk = ((b4 * eg4) * d["kf"]).astype(BF16)
        yv = (b4 * v_ref[rsl(c), gsl(g)].astype(F32)).astype(BF16)
        rhs = jnp.concatenate([_block_diag(yk, GDN_GROUP), _block_diag(yv, GDN_GROUP)], axis=1)
        wu = jnp.dot(d.pop("inv").astype(BF16), rhs, preferred_element_type=F32).astype(BF16)
        d["eg"] = eg4
        d["w"] = wu[:, :gw]
        d["u"] = wu[:, gw:]

    for c, g in insts:
        d = st[c, g]
        att = (d.pop("qk") * d.pop("decay")).astype(BF16)
        rhs = jnp.concatenate([_block_diag(d["w"], GDN_GROUP), _block_diag(d["u"], GDN_GROUP)], axis=1)
        aw = jnp.dot(att, rhs, preferred_element_type=F32)
        d["q_eff"] = (d.pop("qf") * d.pop("eg") - aw[:, :gw]).astype(BF16)
        d["o0"] = aw[:, gw:]

    for c, g in insts:
        d = st[c, g]
        g_end = d["gx"][last:last + 1, :]
        kd = (d.pop("kf") * jnp.exp(g_end - d.pop("gx"))).astype(BF16)
        d["gl"] = jnp.exp(g_end)
        d["m"] = []
        d["b"] = []
        for p in range(npair):
            psl = slice(p * pw, (p + 1) * pw)
            z = lax.dot_general(kd[:, psl], jnp.concatenate([d["w"][:, psl], d["u"][:, psl]], axis=1),
                                tn_dims, preferred_element_type=F32)
            d["m"].append(jnp.where(pair_diag, z[:, :pw], 0.0).astype(BF16))
            d["b"].append(jnp.where(pair_diag, z[:, pw:], 0.0))

    states = [s_scr[i] for i in range(ngroup * npair)]
    for c in order:
        for g in range(ngroup):
            d = st[c, g]
            for p in range(npair):
                i = g * npair + p
                psl = slice(p * pw, (p + 1) * pw)
                state = states[i]
                res = jnp.dot(jnp.concatenate([d["q_eff"][:, psl], d["m"][p]], axis=0),
                              state.astype(BF16), preferred_element_type=F32)
                o_ref[rsl(c), g * gw + p * pw:g * gw + (p + 1) * pw] = res[:chunk] + d["o0"][:, psl]
                states[i] = d["gl"][:, psl] * state - res[chunk:] + d["b"][p]
    for i, state in enumerate(states):
        s_scr[i] = state


def _gdn_expand_matrix(reverse):
    dcol = DN_HEADS if reverse else 0
    bcol = 2 * DN_HEADS + dcol
    e = np.zeros((GB_WIDTH, 2 * DN_K), np.float32)
    for term in range(3):
        for h in range(DN_HEADS):
            e[term * GB_USED + dcol + h, h * DN_DK:(h + 1) * DN_DK] = 1.0
            e[term * GB_USED + bcol + h, DN_K + h * DN_DK:DN_K + (h + 1) * DN_DK] = 1.0
    return jnp.asarray(e, dtype=BF16)


def _gdn_scan(qn, kn, vn, gb, grow, reverse):
    bsz, s, _ = qn.shape
    rows = min(GDN_ROWS, s)
    nt = s // rows
    nchunk = rows // GDN_CHUNK
    expand = _gdn_expand_matrix(reverse)

    def tmap(t):
        return nt - 1 - t if reverse else t

    def tile(width):
        return pl.BlockSpec((None, rows, width), lambda b, t: (b, tmap(t), 0))

    grow_spec = pl.BlockSpec((None, nchunk) + grow.shape[2:], lambda b, t: (b, tmap(t), 0, 0))
    return pl.pallas_call(
        functools.partial(_gdn_kernel, reverse=reverse),
        grid=(bsz, nt),
        in_specs=[tile(DN_K), tile(DN_K), tile(DN_V), tile(GB_WIDTH), grow_spec,
                  _const_spec(expand.shape)],
        out_specs=tile(DN_V),
        out_shape=jax.ShapeDtypeStruct((bsz, s, DN_V), F32),
        scratch_shapes=[pltpu.VMEM((DN_HEADS // GDN_PAIR, GDN_PAIR * DN_DK, GDN_PAIR * DN_DV), F32)],
        compiler_params=_params(2),
        name="gdn_bwd" if reverse else "gdn_fwd",
    )(qn, kn, vn, gb, grow, expand)


def _decay_rows(gb):
    bsz, s, _ = gb.shape
    ngroup = DN_HEADS // GDN_GROUP
    g = gb[:, :, :2 * DN_HEADS].reshape(bsz, s // GDN_CHUNK, GDN_CHUNK, 2, ngroup, GDN_GROUP)
    g = jnp.transpose(g, (0, 1, 3, 4, 5, 2)).reshape(bsz, s // GDN_CHUNK, 2 * ngroup,
                                                     GDN_GROUP * GDN_CHUNK)
    return jnp.pad(g, ((0, 0), (0, 0), (0, SUBLANES - 2 * ngroup), (0, 0)))


def _mix_kernel(x_ref, ao_ref, of_ref, ob_ref, z_ref, nw_ref, watt_ref, wdn_ref, wgate_ref,
                bgate_ref, wo_ref, g_ref, b_ref, ones_ref, o_ref, mix_scr):
    x = x_ref[...]
    xb = x.astype(BF16)
    o = of_ref[...] + ob_ref[...]
    ms = _head_sums(o * o, ones_ref) * (1.0 / DN_DV)
    z = z_ref[...].astype(F32)
    dn = (o * lax.rsqrt(ms + RMS_EPS) * nw_ref[...] * (z * _sigmoid(z))).astype(BF16)
    ao = ao_ref[...]
    for c in range(0, D_MODEL, MIX_COLS):
        cols = slice(c, c + MIX_COLS)
        gcols = slice(D_MODEL + c, D_MODEL + c + MIX_COLS)
        y_att = jnp.dot(ao, watt_ref[:, cols], preferred_element_type=F32)
        y_dn = jnp.dot(dn, wdn_ref[:, cols], preferred_element_type=F32)
        g_att = _sigmoid(jnp.dot(xb, wgate_ref[:, cols], preferred_element_type=F32) + bgate_ref[:, cols])
        g_dn = _sigmoid(jnp.dot(xb, wgate_ref[:, gcols], preferred_element_type=F32) + bgate_ref[:, gcols])
        mix_scr[:, cols] = (g_att * y_att + g_dn * y_dn).astype(BF16)
    m = jnp.dot(mix_scr[...], wo_ref[...], preferred_element_type=F32)
    o_ref[...] = _layer_norm_rows(ALPHA * x + m, g_ref[...], b_ref[...])


def _mix(x2d, ao, of, ob, z, nw, watt, wdn, wgate, bgate, wo, g, b, ones):
    n = x2d.shape[0]
    rows = min(MIX_ROWS, n)

    def tile(width):
        return pl.BlockSpec((rows, width), lambda i: (i, 0))

    consts = (nw, watt, wdn, wgate, bgate, wo, g, b, ones)
    return pl.pallas_call(
        _mix_kernel,
        grid=(n // rows,),
        in_specs=[tile(D_MODEL), tile(ATT_Q), tile(DN_V), tile(DN_V), tile(DN_V)]
                 + [_const_spec(c.shape) for c in consts],
        out_specs=tile(D_MODEL),
        out_shape=jax.ShapeDtypeStruct((n, D_MODEL), F32),
        scratch_shapes=[pltpu.VMEM((rows, D_MODEL), BF16)],
        compiler_params=_params(1),
        name="mix_out",
    )(x2d, ao, of, ob, z, *consts)


def _t5_buckets(rel):
    half = REL_BUCKETS // 2
    ret = (rel > 0).astype(np.int32) * half
    n = np.abs(rel)
    max_exact = half // 2
    large = max_exact + (np.log(np.maximum(n, 1) / max_exact) / np.log(REL_MAX_DIST / max_exact)
                         * (half - max_exact)).astype(np.int32)
    large = np.minimum(large, half - 1)
    return (ret + np.where(n < max_exact, n, large)).astype(np.int32)


def _band_bias(rel_bias):
    i = np.arange(BLOCK)[:, None]
    j = np.arange(3 * BLOCK)[None, :]
    rel = j - BLOCK - i
    bias = jnp.transpose(rel_bias.astype(F32)[_t5_buckets(rel)], (2, 0, 1))
    return jnp.where(jnp.asarray(np.abs(rel) <= WINDOW)[None], bias, MASKED)


def _head_block_ones():
    idx = np.arange(MXU_DIM) // DN_DK
    return jnp.asarray(idx[:, None] == idx[None, :], dtype=BF16)


def _row(v):
    return v.reshape(1, -1).astype(F32)


def _pad_lanes(v, width):
    v = v.reshape(1, -1).astype(F32)
    return jnp.pad(v, ((0, 0), (0, width - v.shape[1])))


def _trunk(x, layers, bias, ones):
    bsz, s, _ = x.shape
    x2d = x.reshape(bsz * s, D_MODEL)
    for lp in layers:
        x2d = _ffn_ln(x2d, *lp["ffn0"])
        x3d = x2d.reshape(bsz, s, D_MODEL)
        qa, ka, va, qn, kn, vn, z, gb = _proj(x3d, *lp["proj"], ones)
        ao = _attention(qa, ka, va, bias, lp["sink"])
        grow = _decay_rows(gb)
        o_fwd = _gdn_scan(qn, kn, vn, gb, grow, reverse=False)
        o_bwd = _gdn_scan(qn, kn, vn, gb, grow, reverse=True)
        flat = lambda a: a.reshape(bsz * s, a.shape[-1])
        x2d = _mix(x2d, flat(ao), flat(o_fwd), flat(o_bwd), flat(z), *lp["mix"], ones)
        x2d = _ffn_ln(x2d, *lp["ffn1"])
    return x2d.reshape(bsz, s, D_MODEL)


def kernel(x_prompt, x_sample, ln_g, ln_b, ffn_w_gate, ffn_w_up, ffn_w_down, w_in, attn_sink, rel_bias,
           dn_conv, dn_a_log, dn_dt_bias, dn_norm_w, w_att_out, w_dn_out, w_gate, b_gate, w_o):
    bias = _band_bias(rel_bias)
    ones = _head_block_ones()
    c_att = ATT_Q + 2 * ATT_KV
    c_dn = c_att + DN_QKV
    c_z = c_dn + DN_V
    layers = []
    for l in range(DEPTH):
        wi = w_in[l]
        wab = jnp.pad(wi[:, c_z:], ((0, 0), (0, GB_WIDTH - 4 * DN_HEADS))).astype(BF16)
        nexpa = _pad_lanes(-jnp.exp(dn_a_log[l].astype(F32)), GB_WIDTH)
        dtb = _pad_lanes(dn_dt_bias[l], GB_WIDTH)
        layers.append({
            "ffn0": (ffn_w_gate[l, 0].astype(BF16), ffn_w_up[l, 0].astype(BF16),
                     ffn_w_down[l, 0].astype(BF16), _row(ln_g[l, 0]), _row(ln_b[l, 0])),
            "ffn1": (ffn_w_gate[l, 1].astype(BF16), ffn_w_up[l, 1].astype(BF16),
                     ffn_w_down[l, 1].astype(BF16), _row(ln_g[l, 2]), _row(ln_b[l, 2])),
            "proj": (wi[:, :c_att].astype(BF16), wi[:, c_att:c_dn].astype(BF16),
                     wi[:, c_dn:c_z].astype(BF16), wab, dn_conv[l].astype(F32), nexpa, dtb),
            "sink": attn_sink[l].astype(F32),
            "mix": (_row(jnp.tile(dn_norm_w[l], DN_HEADS)), w_att_out[l].astype(BF16),
                    w_dn_out[l].astype(BF16), w_gate[l].astype(BF16), _row(b_gate[l]),
                    w_o[l].astype(BF16), _row(ln_g[l, 1]), _row(ln_b[l, 1])),
        })
    y_prompt = _trunk(x_prompt, layers, bias, ones)
    y_sample = _trunk(x_sample, layers, bias, ones)
    return (y_prompt, y_sample)
```

```python
import functools
import math

import numpy as np
import jax
import jax.numpy as jnp
from jax import lax
from jax.experimental import pallas as pl
from jax.experimental.pallas import tpu as pltpu

F32 = jnp.float32
BF16 = jnp.bfloat16

D_MODEL = 1024
DEPTH = 2
ATT_HEADS = 8
ATT_KV_HEADS = 2
ATT_GROUP = ATT_HEADS // ATT_KV_HEADS
HEAD_DIM = 64
WINDOW = 128
BLOCK = 128
REL_BUCKETS = 32
REL_MAX_DIST = 128
DN_HEADS = 8
DN_DK = 64
DN_DV = 64
DN_CONV = 3
D_FF = 2816
ALPHA = (2 * DEPTH) ** 0.25
LN_EPS = 1e-5
RMS_EPS = 1e-6
L2_EPS = 1e-6

ATT_Q = ATT_HEADS * HEAD_DIM
ATT_KV = ATT_KV_HEADS * HEAD_DIM
DN_K = DN_HEADS * DN_DK
DN_V = DN_HEADS * DN_DV
DN_QKV = 2 * DN_K + DN_V

LANES = 128
SUBLANES = 8
MXU_DIM = 256
VMEM_LIMIT_BYTES = 56 * 1024 * 1024

FFN_ROWS = 512
FFN_COLS = 256
PROJ_ROWS = 512
ATT_ROWS = 512
GDN_ROWS = 512
GDN_CHUNK = 64
MIX_ROWS = 512
MIX_COLS = 256
GDN_GROUP = 4
GDN_PAIR = 2
GB_WIDTH = LANES
GB_USED = 4 * DN_HEADS
assert GDN_CHUNK == DN_DK and GDN_GROUP * DN_DK == MXU_DIM and 3 * GB_USED <= GB_WIDTH
MASKED = -1e30


def _const_spec(shape):
    nd = len(shape)
    return pl.BlockSpec(shape, lambda *_: (0,) * nd, pipeline_mode=pl.Buffered(1))


def _params(n_axes):
    return pltpu.CompilerParams(dimension_semantics=("arbitrary",) * n_axes,
                                vmem_limit_bytes=VMEM_LIMIT_BYTES)


def _layer_norm_rows(r, g, b):
    mu = jnp.mean(r, axis=-1, keepdims=True)
    d = r - mu
    var = jnp.mean(d * d, axis=-1, keepdims=True)
    return d * lax.rsqrt(var + LN_EPS) * g + b


def _sigmoid(x):
    return 1.0 / (1.0 + jnp.exp(-x))


def _head_sums(sq, ones_ref):
    hi = sq.astype(BF16)
    lo = (sq - hi.astype(F32)).astype(BF16)
    ones = ones_ref[...]
    parts = []
    for c in range(0, sq.shape[1], MXU_DIM):
        parts.append(jnp.dot(hi[:, c:c + MXU_DIM], ones, preferred_element_type=F32)
                     + jnp.dot(lo[:, c:c + MXU_DIM], ones, preferred_element_type=F32))
    return jnp.concatenate(parts, axis=1)


def _ffn_kernel(x_ref, wg_ref, wu_ref, wd_ref, g_ref, b_ref, o_ref, h_scr):
    x = x_ref[...]
    xb = x.astype(BF16)
    for c in range(0, D_FF, FFN_COLS):
        gate = jnp.dot(xb, wg_ref[:, c:c + FFN_COLS], preferred_element_type=F32)
        up = jnp.dot(xb, wu_ref[:, c:c + FFN_COLS], preferred_element_type=F32)
        h_scr[:, c:c + FFN_COLS] = (gate * _sigmoid(gate) * up).astype(BF16)
    y = jnp.dot(h_scr[...], wd_ref[...], preferred_element_type=F32)
    o_ref[...] = _layer_norm_rows(ALPHA * x + 0.5 * y, g_ref[...], b_ref[...])


def _ffn_ln(x2d, wg, wu, wd, g, b):
    n = x2d.shape[0]
    rows = min(FFN_ROWS, n)
    row_spec = pl.BlockSpec((rows, D_MODEL), lambda i: (i, 0))
    return pl.pallas_call(
        _ffn_kernel,
        grid=(n // rows,),
        in_specs=[row_spec, _const_spec(wg.shape), _const_spec(wu.shape), _const_spec(wd.shape),
                  _const_spec(g.shape), _const_spec(b.shape)],
        out_specs=row_spec,
        out_shape=jax.ShapeDtypeStruct((n, D_MODEL), F32),
        scratch_shapes=[pltpu.VMEM((rows, D_FF), BF16)],
        compiler_params=_params(1),
        name="ffn_ln",
    )(x2d, wg, wu, wd, g, b)


def _proj_kernel(xp_ref, x_ref, xn_ref, watt_ref, wdn_ref, wz_ref, wab_ref, conv_ref,
                 nexpa_ref, dtb_ref, ones_ref,
                 qa_ref, ka_ref, va_ref, qn_ref, kn_ref, vn_ref, z_ref, gb_ref):
    t = pl.program_id(1)
    nt = pl.num_programs(1)
    rows = x_ref.shape[0]
    x = x_ref[...]
    xb = x.astype(BF16)

    att = jnp.dot(xb, watt_ref[...], preferred_element_type=F32)
    qa_ref[...] = (att[:, :ATT_Q] * HEAD_DIM ** -0.5).astype(BF16)
    ka_ref[...] = att[:, ATT_Q:ATT_Q + ATT_KV].astype(BF16)
    va_ref[...] = att[:, ATT_Q + ATT_KV:].astype(BF16)

    z_ref[...] = jnp.dot(xb, wz_ref[...], preferred_element_type=F32).astype(BF16)

    ab = jnp.dot(xb, wab_ref[...], preferred_element_type=F32)
    sp_in = ab + dtb_ref[...]
    softplus = jnp.maximum(sp_in, 0.0) + jnp.log1p(jnp.exp(-jnp.abs(sp_in)))
    lane = lax.broadcasted_iota(jnp.int32, ab.shape, 1)
    gb_ref[...] = jnp.where(lane < 2 * DN_HEADS, nexpa_ref[...] * softplus,
                            jnp.where(lane < 4 * DN_HEADS, _sigmoid(ab), 0.0))

    prev_ok = (t > 0).astype(F32)
    next_ok = (t < nt - 1).astype(F32)
    ext = jnp.concatenate([xp_ref[...] * prev_ok, x, xn_ref[...] * next_ok], axis=0).astype(BF16)
    p = jnp.dot(ext, wdn_ref[...], preferred_element_type=F32)
    n_ext = rows + 2 * SUBLANES
    conv = conv_ref[...]
    c = (pltpu.roll(p, 1, 0) * conv[0:1, :] + p * conv[1:2, :]
         + pltpu.roll(p, n_ext - 1, 0) * conv[2:3, :])
    c = c[SUBLANES:SUBLANES + rows, :]
    c = c * _sigmoid(c)
    q = c[:, :DN_K]
    k = c[:, DN_K:2 * DN_K]
    qn_ref[...] = (q * lax.rsqrt(_head_sums(q * q, ones_ref) + L2_EPS) * DN_DK ** -0.5).astype(BF16)
    kn_ref[...] = (k * lax.rsqrt(_head_sums(k * k, ones_ref) + L2_EPS)).astype(BF16)
    vn_ref[...] = c[:, 2 * DN_K:].astype(BF16)


def _proj(x3d, watt, wdn, wz, wab, conv, nexpa, dtb, ones):
    bsz, s, _ = x3d.shape
    rows = min(PROJ_ROWS, s)
    nblk = s // SUBLANES
    rb = rows // SUBLANES

    def tile(width):
        return pl.BlockSpec((None, rows, width), lambda b, t: (b, t, 0))

    prev_spec = pl.BlockSpec((None, SUBLANES, D_MODEL),
                             lambda b, t: (b, jnp.maximum(t * rb - 1, 0), 0))
    next_spec = pl.BlockSpec((None, SUBLANES, D_MODEL),
                             lambda b, t: (b, jnp.minimum((t + 1) * rb, nblk - 1), 0))
    widths = (ATT_Q, ATT_KV, ATT_KV, DN_K, DN_K, DN_V, DN_V)
    out_shape = [jax.ShapeDtypeStruct((bsz, s, w), BF16) for w in widths]
    out_shape.append(jax.ShapeDtypeStruct((bsz, s, GB_WIDTH), F32))
    out_specs = [tile(w) for w in widths] + [tile(GB_WIDTH)]
    consts = (watt, wdn, wz, wab, conv, nexpa, dtb, ones)
    return pl.pallas_call(
        _proj_kernel,
        grid=(bsz, s // rows),
        in_specs=[prev_spec, tile(D_MODEL), next_spec] + [_const_spec(c.shape) for c in consts],
        out_specs=out_specs,
        out_shape=out_shape,
        compiler_params=_params(2),
        name="in_proj",
    )(x3d, x3d, x3d, *consts)


def _attn_kernel(sink_ref, q_ref, kp_ref, k_ref, kn_ref, vp_ref, v_ref, vn_ref, bias_ref, o_ref):
    t = pl.program_id(1)
    nt = pl.num_programs(1)
    rows = q_ref.shape[0]
    nblk = rows // BLOCK
    kcat = jnp.concatenate([kp_ref[...], k_ref[...], kn_ref[...]], axis=0)
    vcat = jnp.concatenate([vp_ref[...], v_ref[...], vn_ref[...]], axis=0)
    col = lax.broadcasted_iota(jnp.int32, (BLOCK, 3 * BLOCK), 1)
    first_tile = t == 0
    last_tile = t == nt - 1
    for j in range(nblk):
        r0 = j * BLOCK
        edge = None
        if j == 0:
            edge = jnp.logical_and(first_tile, col < BLOCK)
        if j == nblk - 1:
            hi_edge = jnp.logical_and(last_tile, col >= 2 * BLOCK)
            edge = hi_edge if edge is None else jnp.logical_or(edge, hi_edge)
        heads = range(ATT_HEADS)
        kws = [kcat[r0:r0 + 3 * BLOCK, kh * HEAD_DIM:(kh + 1) * HEAD_DIM] for kh in range(ATT_KV_HEADS)]
        vws = [vcat[r0:r0 + 3 * BLOCK, kh * HEAD_DIM:(kh + 1) * HEAD_DIM] for kh in range(ATT_KV_HEADS)]
        logits = []
        for h in heads:
            q = q_ref[r0:r0 + BLOCK, h * HEAD_DIM:(h + 1) * HEAD_DIM]
            s = lax.dot_general(q, kws[h // ATT_GROUP], (((1,), (1,)), ((), ())),
                                preferred_element_type=F32) + bias_ref[h]
            logits.append(s if edge is None else jnp.where(edge, MASKED, s))
        tops = [jnp.maximum(jnp.max(logits[h], axis=-1, keepdims=True), sink_ref[h]) for h in heads]
        probs = [jnp.exp(logits[h] - tops[h]) for h in heads]
        denoms = [jnp.sum(probs[h], axis=-1, keepdims=True) + jnp.exp(sink_ref[h] - tops[h])
                  for h in heads]
        outs = [jnp.dot(probs[h].astype(BF16), vws[h // ATT_GROUP], preferred_element_type=F32)
                / denoms[h] for h in heads]
        heads_per_store = LANES // HEAD_DIM
        for h in range(0, ATT_HEADS, heads_per_store):
            o_ref[r0:r0 + BLOCK, h * HEAD_DIM:(h + heads_per_store) * HEAD_DIM] = jnp.concatenate(
                outs[h:h + heads_per_store], axis=1).astype(BF16)


def _attention(qa, ka, va, bias, sink):
    bsz, s, _ = qa.shape
    rows = min(ATT_ROWS, s)
    rb = rows // BLOCK
    nblk = s // BLOCK

    def tile(width):
        return pl.BlockSpec((None, rows, width), lambda b, t: (b, t, 0))

    prev_spec = pl.BlockSpec((None, BLOCK, ATT_KV), lambda b, t: (b, jnp.maximum(t * rb - 1, 0), 0))
    next_spec = pl.BlockSpec((None, BLOCK, ATT_KV),
                             lambda b, t: (b, jnp.minimum((t + 1) * rb, nblk - 1), 0))
    return pl.pallas_call(
        _attn_kernel,
        grid=(bsz, s // rows),
        in_specs=[pl.BlockSpec(memory_space=pltpu.SMEM),
                  tile(ATT_Q), prev_spec, tile(ATT_KV), next_spec,
                  prev_spec, tile(ATT_KV), next_spec, _const_spec(bias.shape)],
        out_specs=tile(ATT_Q),
        out_shape=jax.ShapeDtypeStruct((bsz, s, ATT_Q), BF16),
        compiler_params=_params(2),
        name="window_attn",
    )(sink, qa, ka, ka, ka, va, va, va, bias)


def _segmented_cumsum(x, axis, seg, reverse):
    n = x.shape[axis]
    pos = lax.broadcasted_iota(jnp.int32, x.shape, axis) % seg
    step = 1
    while step < seg:
        if reverse:
            shifted = pltpu.roll(x, n - step, axis)
            keep = pos < seg - step
        else:
            shifted = pltpu.roll(x, step, axis)
            keep = pos >= step
        x = x + jnp.where(keep, shifted, 0.0)
        step *= 2
    return x


def _block_diag(xb, nblk):
    head = lax.broadcasted_iota(jnp.int32, xb.shape, 1) // DN_DK
    zero = jnp.zeros_like(xb)
    return jnp.concatenate([jnp.where(head == g, xb, zero) for g in range(nblk)], axis=0)


def _split3_packed(x):
    t1 = x.astype(BF16).astype(F32)
    r1 = x - t1
    t2 = r1.astype(BF16).astype(F32)
    t3 = (r1 - t2).astype(BF16).astype(F32)
    return (t1 + pltpu.roll(t2, GB_USED, 1) + pltpu.roll(t3, 2 * GB_USED, 1)).astype(BF16)


def _gdn_kernel(q_ref, k_ref, v_ref, gb_ref, grow_ref, expand_ref, o_ref, s_scr, *, reverse):
    t = pl.program_id(1)
    rows = q_ref.shape[0]
    chunk = GDN_CHUNK
    nchunk = rows // chunk
    gw = GDN_GROUP * DN_DK
    pw = GDN_PAIR * DN_DK
    ngroup = DN_HEADS // GDN_GROUP
    npair = GDN_GROUP // GDN_PAIR

    @pl.when(t == 0)
    def _():
        s_scr[...] = jnp.zeros_like(s_scr)

    gb = gb_ref[...]
    lane = lax.broadcasted_iota(jnp.int32, gb.shape, 1)
    scal = jnp.where(lane < 2 * DN_HEADS, _segmented_cumsum(gb, 0, chunk, reverse), gb)
    spread = jnp.dot(_split3_packed(scal), expand_ref[...], preferred_element_type=F32)
    gx = spread[:, :DN_K]
    bx = spread[:, DN_K:]
    egx = jnp.exp(gx)

    ri = lax.broadcasted_iota(jnp.int32, (chunk, gw), 0)
    cj = lax.broadcasted_iota(jnp.int32, (chunk, gw), 1) % chunk
    incl = (ri <= cj) if reverse else (ri >= cj)
    strict = (ri < cj) if reverse else (ri > cj)
    eye = (ri == cj).astype(F32)
    pair_diag = (lax.broadcasted_iota(jnp.int32, (pw, pw), 0) // DN_DK
                 == lax.broadcasted_iota(jnp.int32, (pw, pw), 1) // DN_DK)
    last = 0 if reverse else chunk - 1
    order = list(range(nchunk - 1, -1, -1)) if reverse else list(range(nchunk))
    insts = [(c, g) for c in order for g in range(ngroup)]
    nt_dims = (((1,), (1,)), ((), ()))
    tn_dims = (((0,), (0,)), ((), ()))
    grow_base = ngroup if reverse else 0

    def rsl(c):
        return slice(c * chunk, (c + 1) * chunk)

    def gsl(g):
        return slice(g * gw, (g + 1) * gw)

    st = {key: {} for key in insts}

    for c, g in insts:
        d = st[c, g]
        kb = k_ref[rsl(c), gsl(g)]
        qb = q_ref[rsl(c), gsl(g)]
        kq = lax.dot_general(jnp.concatenate([kb, qb], axis=0), _block_diag(kb, GDN_GROUP), nt_dims,
                             preferred_element_type=F32)
        d["kf"] = kb.astype(F32)
        d["qf"] = qb.astype(F32)
        d["kk"] = kq[:chunk]
        d["qk"] = kq[chunk:]

    for c, g in insts:
        d = st[c, g]
        gx4 = gx[rsl(c), gsl(g)]
        grow = _segmented_cumsum(grow_ref[c], 1, chunk, reverse)[grow_base + g:grow_base + g + 1, :]
        decay = jnp.where(incl, jnp.exp(jnp.where(incl, gx4 - grow, 0.0)), 0.0)
        d["gx"] = gx4
        d["decay"] = decay
        d["neg_a"] = jnp.where(strict, -(bx[rsl(c), gsl(g)] * d.pop("kk") * decay), 0.0)

    for c, g in insts:
        d = st[c, g]
        d["inv"] = eye + d["neg_a"]
        pb = d.pop("neg_a").astype(BF16)
        d["power"] = jnp.dot(pb, _block_diag(pb, GDN_GROUP), preferred_element_type=F32)
    step = 4
    while step < chunk:
        for c, g in insts:
            d = st[c, g]
            pb = d["power"].astype(BF16)
            both = jnp.dot(jnp.concatenate([d["inv"].astype(BF16), pb], axis=0),
                           _block_diag(pb, GDN_GROUP), preferred_element_type=F32)
            d["inv"] = d["inv"] + both[:chunk]
            d["power"] = both[chunk:]
        step *= 2
    for c, g in insts:
        d = st[c, g]
        pb = d.pop("power").astype(BF16)
        d["inv"] = d["inv"] + jnp.dot(d["inv"].astype(BF16), _block_diag(pb, GDN_GROUP),
                                      preferred_element_type=F32)

    for c, g in insts:
        d = st[c, g]
        att = (d.pop("qk") * d.pop("decay")).astype(BF16)
        d["inv"] = d["inv"].astype(BF16)
        d["att_inv"] = jnp.dot(att, _block_diag(d["inv"], GDN_GROUP),
                               preferred_element_type=F32).astype(BF16)

    for c, g in insts:
        d = st[c, g]
        b4 = bx[rsl(c), gsl(g)]
        eg4 = egx[rsl(c), gsl(g)]
        yk = ((b4 * eg4) * d["kf"]).astype(BF16)
        yv = (b4 * v_ref[rsl(c), gsl(g)].astype(F32)).astype(BF16)
        rhs = jnp.concatenate([_block_diag(yk, GDN_GROUP), _block_diag(yv, GDN_GROUP)], axis=1)
        both = jnp.dot(jnp.concatenate([d.pop("inv"), d.pop("att_inv")], axis=0), rhs,
                       preferred_element_type=F32)
        d["w"] = both[:chunk, :gw].astype(BF16)
        d["u"] = both[:chunk, gw:].astype(BF16)
        d["q_eff"] = (d.pop("qf") * eg4 - both[chunk:, :gw]).astype(BF16)
        d["o0"] = both[chunk:, gw:]

    for c, g in insts:
        d = st[c, g]
        g_end = d["gx"][last:last + 1, :]
        kd = (d.pop("kf") * jnp.exp(g_end - d.pop("gx"))).astype(BF16)
        d["gl"] = jnp.exp(g_end)
        d["m"] = []
        d["b"] = []
        for p in range(npair):
            psl = slice(p * pw, (p + 1) * pw)
            z = lax.dot_general(kd[:, psl], jnp.concatenate([d["w"][:, psl], d["u"][:, psl]], axis=1),
                                tn_dims, preferred_element_type=F32)
            d["m"].append(jnp.where(pair_diag, z[:, :pw], 0.0).astype(BF16))
            d["b"].append(jnp.where(pair_diag, z[:, pw:], 0.0))

    states = [s_scr[i] for i in range(ngroup * npair)]
    for c in order:
        for g in range(ngroup):
            d = st[c, g]
            for p in range(npair):
                i = g * npair + p
                psl = slice(p * pw, (p + 1) * pw)
                state = states[i]
                res = jnp.dot(jnp.concatenate([d["q_eff"][:, psl], d["m"][p]], axis=0),
                              state.astype(BF16), preferred_element_type=F32)
                o_ref[rsl(c), g * gw + p * pw:g * gw + (p + 1) * pw] = res[:chunk] + d["o0"][:, psl]
                states[i] = d["gl"][:, psl] * state - res[chunk:] + d["b"][p]
    for i, state in enumerate(states):
        s_scr[i] = state


def _gdn_expand_matrix(reverse):
    dcol = DN_HEADS if reverse else 0
    bcol = 2 * DN_HEADS + dcol
    e = np.zeros((GB_WIDTH, 2 * DN_K), np.float32)
    for term in range(3):
        for h in range(DN_HEADS):
            e[term * GB_USED + dcol + h, h * DN_DK:(h + 1) * DN_DK] = 1.0
            e[term * GB_USED + bcol + h, DN_K + h * DN_DK:DN_K + (h + 1) * DN_DK] = 1.0
    return jnp.asarray(e, dtype=BF16)


def _gdn_scan(qn, kn, vn, gb, grow, reverse):
    bsz, s, _ = qn.shape
    rows = min(GDN_ROWS, s)
    nt = s // rows
    nchunk = rows // GDN_CHUNK
    expand = _gdn_expand_matrix(reverse)

    def tmap(t):
        return nt - 1 - t if reverse else t

    def tile(width):
        return pl.BlockSpec((None, rows, width), lambda b, t: (b, tmap(t), 0))

    grow_spec = pl.BlockSpec((None, nchunk) + grow.shape[2:], lambda b, t: (b, tmap(t), 0, 0))
    return pl.pallas_call(
        functools.partial(_gdn_kernel, reverse=reverse),
        grid=(bsz, nt),
        in_specs=[tile(DN_K), tile(DN_K), tile(DN_V), tile(GB_WIDTH), grow_spec,
                  _const_spec(expand.shape)],
        out_specs=tile(DN_V),
        out_shape=jax.ShapeDtypeStruct((bsz, s, DN_V), F32),
        scratch_shapes=[pltpu.VMEM((DN_HEADS // GDN_PAIR, GDN_PAIR * DN_DK, GDN_PAIR * DN_DV), F32)],
        compiler_params=_params(2),
        name="gdn_bwd" if reverse else "gdn_fwd",
    )(qn, kn, vn, gb, grow, expand)


def _decay_rows(gb):
    bsz, s, _ = gb.shape
    ngroup = DN_HEADS // GDN_GROUP
    g = gb[:, :, :2 * DN_HEADS].reshape(bsz, s // GDN_CHUNK, GDN_CHUNK, 2, ngroup, GDN_GROUP)
    g = jnp.transpose(g, (0, 1, 3, 4, 5, 2)).reshape(bsz, s // GDN_CHUNK, 2 * ngroup,
                                                     GDN_GROUP * GDN_CHUNK)
    return jnp.pad(g, ((0, 0), (0, 0), (0, SUBLANES - 2 * ngroup), (0, 0)))


def _mix_kernel(x_ref, ao_ref, of_ref, ob_ref, z_ref, nw_ref, watt_ref, wdn_ref, wgate_ref,
                bgate_ref, wo_ref, g_ref, b_ref, ones_ref, o_ref, mix_scr):
    x = x_ref[...]
    xb = x.astype(BF16)
    o = of_ref[...] + ob_ref[...]
    ms = _head_sums(o * o, ones_ref) * (1.0 / DN_DV)
    z = z_ref[...].astype(F32)
    dn = (o * lax.rsqrt(ms + RMS_EPS) * nw_ref[...] * (z * _sigmoid(z))).astype(BF16)
    ao = ao_ref[...]
    for c in range(0, D_MODEL, MIX_COLS):
        cols = slice(c, c + MIX_COLS)
        gcols = slice(D_MODEL + c, D_MODEL + c + MIX_COLS)
        y_att = jnp.dot(ao, watt_ref[:, cols], preferred_element_type=F32)
        y_dn = jnp.dot(dn, wdn_ref[:, cols], preferred_element_type=F32)
        g_att = _sigmoid(jnp.dot(xb, wgate_ref[:, cols], preferred_element_type=F32) + bgate_ref[:, cols])
        g_dn = _sigmoid(jnp.dot(xb, wgate_ref[:, gcols], preferred_element_type=F32) + bgate_ref[:, gcols])
        mix_scr[:, cols] = (g_att * y_att + g_dn * y_dn).astype(BF16)
    m = jnp.dot(mix_scr[...], wo_ref[...], preferred_element_type=F32)
    o_ref[...] = _layer_norm_rows(ALPHA * x + m, g_ref[...], b_ref[...])


def _mix(x2d, ao, of, ob, z, nw, watt, wdn, wgate, bgate, wo, g, b, ones):
    n = x2d.shape[0]
    rows = min(MIX_ROWS, n)

    def tile(width):
        return pl.BlockSpec((rows, width), lambda i: (i, 0))

    consts = (nw, watt, wdn, wgate, bgate, wo, g, b, ones)
    return pl.pallas_call(
        _mix_kernel,
        grid=(n // rows,),
        in_specs=[tile(D_MODEL), tile(ATT_Q), tile(DN_V), tile(DN_V), tile(DN_V)]
                 + [_const_spec(c.shape) for c in consts],
        out_specs=tile(D_MODEL),
        out_shape=jax.ShapeDtypeStruct((n, D_MODEL), F32),
        scratch_shapes=[pltpu.VMEM((rows, D_MODEL), BF16)],
        compiler_params=_params(1),
        name="mix_out",
    )(x2d, ao, of, ob, z, *consts)


def _t5_buckets(rel):
    half = REL_BUCKETS // 2
    ret = (rel > 0).astype(np.int32) * half
    n = np.abs(rel)
    max_exact = half // 2
    large = max_exact + (np.log(np.maximum(n, 1) / max_exact) / np.log(REL_MAX_DIST / max_exact)
                         * (half - max_exact)).astype(np.int32)
    large = np.minimum(large, half - 1)
    return (ret + np.where(n < max_exact, n, large)).astype(np.int32)


def _band_bias(rel_bias):
    i = np.arange(BLOCK)[:, None]
    j = np.arange(3 * BLOCK)[None, :]
    rel = j - BLOCK - i
    select = jnp.asarray(_t5_buckets(rel).reshape(-1, 1) == np.arange(REL_BUCKETS)[None, :], dtype=F32)
    bias = jnp.dot(select, rel_bias.astype(F32), precision=lax.Precision.HIGHEST)
    bias = jnp.transpose(bias.reshape(BLOCK, 3 * BLOCK, ATT_HEADS), (2, 0, 1))
    return jnp.where(jnp.asarray(np.abs(rel) <= WINDOW)[None], bias, MASKED)


def _head_block_ones():
    idx = np.arange(MXU_DIM) // DN_DK
    return jnp.asarray(idx[:, None] == idx[None, :], dtype=BF16)


def _row(v):
    return v.reshape(1, -1).astype(F32)


def _pad_lanes(v, width):
    v = v.reshape(1, -1).astype(F32)
    return jnp.pad(v, ((0, 0), (0, width - v.shape[1])))


def _trunk(x, layers, bias, ones):
    bsz, s, _ = x.shape
    x2d = x.reshape(bsz * s, D_MODEL)
    for lp in layers:
        x2d = _ffn_ln(x2d, *lp["ffn0"])
        x3d = x2d.reshape(bsz, s, D_MODEL)
        qa, ka, va, qn, kn, vn, z, gb = _proj(x3d, *lp["proj"], ones)
        ao = _attention(qa, ka, va, bias, lp["sink"])
        grow = _decay_rows(gb)
        o_fwd = _gdn_scan(qn, kn, vn, gb, grow, reverse=False)
        o_bwd = _gdn_scan(qn, kn, vn, gb, grow, reverse=True)
        flat = lambda a: a.reshape(bsz * s, a.shape[-1])
        x2d = _mix(x2d, flat(ao), flat(o_fwd), flat(o_bwd), flat(z), *lp["mix"], ones)
        x2d = _ffn_ln(x2d, *lp["ffn1"])
    return x2d.reshape(bsz, s, D_MODEL)


def kernel(x_prompt, x_sample, ln_g, ln_b, ffn_w_gate, ffn_w_up, ffn_w_down, w_in, attn_sink, rel_bias,
           dn_conv, dn_a_log, dn_dt_bias, dn_norm_w, w_att_out, w_dn_out, w_gate, b_gate, w_o):
    bias = _band_bias(rel_bias)
    ones = _head_block_ones()
    c_att = ATT_Q + 2 * ATT_KV
    c_dn = c_att + DN_QKV
    c_z = c_dn + DN_V
    layers = []
    for l in range(DEPTH):
        wi = w_in[l]
        wab = jnp.pad(wi[:, c_z:], ((0, 0), (0, GB_WIDTH - 4 * DN_HEADS))).astype(BF16)
        nexpa = _pad_lanes(-jnp.exp(dn_a_log[l].astype(F32)), GB_WIDTH)
        dtb = _pad_lanes(dn_dt_bias[l], GB_WIDTH)
        layers.append({
            "ffn0": (ffn_w_gate[l, 0].astype(BF16), ffn_w_up[l, 0].astype(BF16),
                     ffn_w_down[l, 0].astype(BF16), _row(ln_g[l, 0]), _row(ln_b[l, 0])),
            "ffn1": (ffn_w_gate[l, 1].astype(BF16), ffn_w_up[l, 1].astype(BF16),
                     ffn_w_down[l, 1].astype(BF16), _row(ln_g[l, 2]), _row(ln_b[l, 2])),
            "proj": (wi[:, :c_att].astype(BF16), wi[:, c_att:c_dn].astype(BF16),
                     wi[:, c_dn:c_z].astype(BF16), wab, dn_conv[l].astype(F32), nexpa, dtb),
            "sink": attn_sink[l].astype(F32),
            "mix": (_row(jnp.tile(dn_norm_w[l], DN_HEADS)), w_att_out[l].astype(BF16),
                    w_dn_out[l].astype(BF16), w_gate[l].astype(BF16), _row(b_gate[l]),
                    w_o[l].astype(BF16), _row(ln_g[l, 1]), _row(ln_b[l, 1])),
        })
    y_prompt = _trunk(x_prompt, layers, bias, ones)
    y_sample = _trunk(x_sample, layers, bias, ones)
    return (y_prompt, y_sample)
```

```python
import functools
import math

import numpy as np
import jax
import jax.numpy as jnp
from jax import lax
from jax.experimental import pallas as pl
from jax.experimental.pallas import tpu as pltpu

F32 = jnp.float32
BF16 = jnp.bfloat16

D_MODEL = 1024
DEPTH = 2
ATT_HEADS = 8
ATT_KV_HEADS = 2
ATT_GROUP = ATT_HEADS // ATT_KV_HEADS
HEAD_DIM = 64
WINDOW = 128
BLOCK = 128
REL_BUCKETS = 32
REL_MAX_DIST = 128
DN_HEADS = 8
DN_DK = 64
DN_DV = 64
DN_CONV = 3
D_FF = 2816
ALPHA = (2 * DEPTH) ** 0.25
LN_EPS = 1e-5
RMS_EPS = 1e-6
L2_EPS = 1e-6

ATT_Q = ATT_HEADS * HEAD_DIM
ATT_KV = ATT_KV_HEADS * HEAD_DIM
DN_K = DN_HEADS * DN_DK
DN_V = DN_HEADS * DN_DV
DN_QKV = 2 * DN_K + DN_V

LANES = 128
SUBLANES = 8
MXU_DIM = 256
VMEM_LIMIT_BYTES = 56 * 1024 * 1024

FFN_ROWS = 1024
FFN_COLS = 256
PROJ_ROWS = 512
ATT_ROWS = 512
GDN_ROWS = 512
GDN_CHUNK = 64
MIX_ROWS = 512
MIX_COLS = 256
GDN_GROUP = 4
GDN_PAIR = 2
GB_WIDTH = LANES
GB_USED = 4 * DN_HEADS
assert GDN_CHUNK == DN_DK and GDN_GROUP * DN_DK == MXU_DIM and 3 * GB_USED <= GB_WIDTH
MASKED = -1e30


def _const_spec(shape):
    nd = len(shape)
    return pl.BlockSpec(shape, lambda *_: (0,) * nd, pipeline_mode=pl.Buffered(1))


def _params(n_axes):
    return pltpu.CompilerParams(dimension_semantics=("arbitrary",) * n_axes,
                                vmem_limit_bytes=VMEM_LIMIT_BYTES)


def _layer_norm_rows(r, g, b):
    mu = jnp.mean(r, axis=-1, keepdims=True)
    d = r - mu
    var = jnp.mean(d * d, axis=-1, keepdims=True)
    return d * lax.rsqrt(var + LN_EPS) * g + b


def _sigmoid(x):
    return 1.0 / (1.0 + jnp.exp(-x))


def _head_sums(sq, ones_ref):
    hi = sq.astype(BF16)
    lo = (sq - hi.astype(F32)).astype(BF16)
    ones = ones_ref[...]
    parts = []
    for c in range(0, sq.shape[1], MXU_DIM):
        parts.append(jnp.dot(hi[:, c:c + MXU_DIM], ones, preferred_element_type=F32)
                     + jnp.dot(lo[:, c:c + MXU_DIM], ones, preferred_element_type=F32))
    return jnp.concatenate(parts, axis=1)


def _ffn_kernel(x_ref, wg_ref, wu_ref, wd_ref, g_ref, b_ref, o_ref, h_scr):
    x = x_ref[...]
    xb = x.astype(BF16)
    for c in range(0, D_FF, FFN_COLS):
        gate = jnp.dot(xb, wg_ref[:, c:c + FFN_COLS], preferred_element_type=F32)
        up = jnp.dot(xb, wu_ref[:, c:c + FFN_COLS], preferred_element_type=F32)
        h_scr[:, c:c + FFN_COLS] = (gate * _sigmoid(gate) * up).astype(BF16)
    y = jnp.dot(h_scr[...], wd_ref[...], preferred_element_type=F32)
    o_ref[...] = _layer_norm_rows(ALPHA * x + 0.5 * y, g_ref[...], b_ref[...])


def _ffn_ln(x2d, wg, wu, wd, g, b):
    n = x2d.shape[0]
    rows = min(FFN_ROWS, n)
    row_spec = pl.BlockSpec((rows, D_MODEL), lambda i: (i, 0))
    return pl.pallas_call(
        _ffn_kernel,
        grid=(n // rows,),
        in_specs=[row_spec, _const_spec(wg.shape), _const_spec(wu.shape), _const_spec(wd.shape),
                  _const_spec(g.shape), _const_spec(b.shape)],
        out_specs=row_spec,
        out_shape=jax.ShapeDtypeStruct((n, D_MODEL), F32),
        scratch_shapes=[pltpu.VMEM((rows, D_FF), BF16)],
        compiler_params=_params(1),
        name="ffn_ln",
    )(x2d, wg, wu, wd, g, b)


def _proj_kernel(xp_ref, x_ref, xn_ref, watt_ref, wdn_ref, wz_ref, wab_ref, conv_ref,
                 nexpa_ref, dtb_ref, ones_ref,
                 qa_ref, ka_ref, va_ref, qn_ref, kn_ref, vn_ref, z_ref, gb_ref):
    t = pl.program_id(1)
    nt = pl.num_programs(1)
    rows = x_ref.shape[0]
    x = x_ref[...]
    xb = x.astype(BF16)

    att = jnp.dot(xb, watt_ref[...], preferred_element_type=F32)
    qa_ref[...] = (att[:, :ATT_Q] * HEAD_DIM ** -0.5).astype(BF16)
    ka_ref[...] = att[:, ATT_Q:ATT_Q + ATT_KV].astype(BF16)
    va_ref[...] = att[:, ATT_Q + ATT_KV:].astype(BF16)

    z_ref[...] = jnp.dot(xb, wz_ref[...], preferred_element_type=F32).astype(BF16)

    ab = jnp.dot(xb, wab_ref[...], preferred_element_type=F32)
    sp_in = ab + dtb_ref[...]
    softplus = jnp.maximum(sp_in, 0.0) + jnp.log1p(jnp.exp(-jnp.abs(sp_in)))
    lane = lax.broadcasted_iota(jnp.int32, ab.shape, 1)
    gb_ref[...] = jnp.where(lane < 2 * DN_HEADS, nexpa_ref[...] * softplus,
                            jnp.where(lane < 4 * DN_HEADS, _sigmoid(ab), 0.0))

    prev_ok = (t > 0).astype(F32)
    next_ok = (t < nt - 1).astype(F32)
    ext = jnp.concatenate([xp_ref[...] * prev_ok, x, xn_ref[...] * next_ok], axis=0).astype(BF16)
    p = jnp.dot(ext, wdn_ref[...], preferred_element_type=F32)
    n_ext = rows + 2 * SUBLANES
    conv = conv_ref[...]
    c = (pltpu.roll(p, 1, 0) * conv[0:1, :] + p * conv[1:2, :]
         + pltpu.roll(p, n_ext - 1, 0) * conv[2:3, :])
    c = c[SUBLANES:SUBLANES + rows, :]
    c = c * _sigmoid(c)
    q = c[:, :DN_K]
    k = c[:, DN_K:2 * DN_K]
    qn_ref[...] = (q * lax.rsqrt(_head_sums(q * q, ones_ref) + L2_EPS) * DN_DK ** -0.5).astype(BF16)
    kn_ref[...] = (k * lax.rsqrt(_head_sums(k * k, ones_ref) + L2_EPS)).astype(BF16)
    vn_ref[...] = c[:, 2 * DN_K:].astype(BF16)


def _proj(x3d, watt, wdn, wz, wab, conv, nexpa, dtb, ones):
    bsz, s, _ = x3d.shape
    rows = min(PROJ_ROWS, s)
    nblk = s // SUBLANES
    rb = rows // SUBLANES

    def tile(width):
        return pl.BlockSpec((None, rows, width), lambda b, t: (b, t, 0))

    prev_spec = pl.BlockSpec((None, SUBLANES, D_MODEL),
                             lambda b, t: (b, jnp.maximum(t * rb - 1, 0), 0))
    next_spec = pl.BlockSpec((None, SUBLANES, D_MODEL),
                             lambda b, t: (b, jnp.minimum((t + 1) * rb, nblk - 1), 0))
    widths = (ATT_Q, ATT_KV, ATT_KV, DN_K, DN_K, DN_V, DN_V)
    out_shape = [jax.ShapeDtypeStruct((bsz, s, w), BF16) for w in widths]
    out_shape.append(jax.ShapeDtypeStruct((bsz, s, GB_WIDTH), F32))
    out_specs = [tile(w) for w in widths] + [tile(GB_WIDTH)]
    consts = (watt, wdn, wz, wab, conv, nexpa, dtb, ones)
    return pl.pallas_call(
        _proj_kernel,
        grid=(bsz, s // rows),
        in_specs=[prev_spec, tile(D_MODEL), next_spec] + [_const_spec(c.shape) for c in consts],
        out_specs=out_specs,
        out_shape=out_shape,
        compiler_params=_params(2),
        name="in_proj",
    )(x3d, x3d, x3d, *consts)


def _attn_kernel(sink_ref, q_ref, kp_ref, k_ref, kn_ref, vp_ref, v_ref, vn_ref, bias_ref, o_ref):
    t = pl.program_id(1)
    nt = pl.num_programs(1)
    rows = q_ref.shape[0]
    nblk = rows // BLOCK
    kcat = jnp.concatenate([kp_ref[...], k_ref[...], kn_ref[...]], axis=0)
    vcat = jnp.concatenate([vp_ref[...], v_ref[...], vn_ref[...]], axis=0)
    col = lax.broadcasted_iota(jnp.int32, (BLOCK, 3 * BLOCK), 1)
    first_tile = t == 0
    last_tile = t == nt - 1
    for j in range(nblk):
        r0 = j * BLOCK
        edge = None
        if j == 0:
            edge = jnp.logical_and(first_tile, col < BLOCK)
        if j == nblk - 1:
            hi_edge = jnp.logical_and(last_tile, col >= 2 * BLOCK)
            edge = hi_edge if edge is None else jnp.logical_or(edge, hi_edge)
        heads = range(ATT_HEADS)
        kws = [kcat[r0:r0 + 3 * BLOCK, kh * HEAD_DIM:(kh + 1) * HEAD_DIM] for kh in range(ATT_KV_HEADS)]
        vws = [vcat[r0:r0 + 3 * BLOCK, kh * HEAD_DIM:(kh + 1) * HEAD_DIM] for kh in range(ATT_KV_HEADS)]
        logits = []
        for h in heads:
            q = q_ref[r0:r0 + BLOCK, h * HEAD_DIM:(h + 1) * HEAD_DIM]
            s = lax.dot_general(q, kws[h // ATT_GROUP], (((1,), (1,)), ((), ())),
                                preferred_element_type=F32) + bias_ref[h]
            logits.append(s if edge is None else jnp.where(edge, MASKED, s))
        tops = [jnp.maximum(jnp.max(logits[h], axis=-1, keepdims=True), sink_ref[h]) for h in heads]
        probs = [jnp.exp(logits[h] - tops[h]) for h in heads]
        denoms = [jnp.sum(probs[h], axis=-1, keepdims=True) + jnp.exp(sink_ref[h] - tops[h])
                  for h in heads]
        outs = [jnp.dot(probs[h].astype(BF16), vws[h // ATT_GROUP], preferred_element_type=F32)
                / denoms[h] for h in heads]
        heads_per_store = LANES // HEAD_DIM
        for h in range(0, ATT_HEADS, heads_per_store):
            o_ref[r0:r0 + BLOCK, h * HEAD_DIM:(h + heads_per_store) * HEAD_DIM] = jnp.concatenate(
                outs[h:h + heads_per_store], axis=1).astype(BF16)


def _attention(qa, ka, va, bias, sink):
    bsz, s, _ = qa.shape
    rows = min(ATT_ROWS, s)
    rb = rows // BLOCK
    nblk = s // BLOCK

    def tile(width):
        return pl.BlockSpec((None, rows, width), lambda b, t: (b, t, 0))

    prev_spec = pl.BlockSpec((None, BLOCK, ATT_KV), lambda b, t: (b, jnp.maximum(t * rb - 1, 0), 0))
    next_spec = pl.BlockSpec((None, BLOCK, ATT_KV),
                             lambda b, t: (b, jnp.minimum((t + 1) * rb, nblk - 1), 0))
    return pl.pallas_call(
        _attn_kernel,
        grid=(bsz, s // rows),
        in_specs=[pl.BlockSpec(memory_space=pltpu.SMEM),
                  tile(ATT_Q), prev_spec, tile(ATT_KV), next_spec,
                  prev_spec, tile(ATT_KV), next_spec, _const_spec(bias.shape)],
        out_specs=tile(ATT_Q),
        out_shape=jax.ShapeDtypeStruct((bsz, s, ATT_Q), BF16),
        compiler_params=_params(2),
        name="window_attn",
    )(sink, qa, ka, ka, ka, va, va, va, bias)


def _segmented_cumsum(x, axis, seg, reverse):
    n = x.shape[axis]
    pos = lax.broadcasted_iota(jnp.int32, x.shape, axis) % seg
    step = 1
    while step < seg:
        if reverse:
            shifted = pltpu.roll(x, n - step, axis)
            keep = pos < seg - step
        else:
            shifted = pltpu.roll(x, step, axis)
            keep = pos >= step
        x = x + jnp.where(keep, shifted, 0.0)
        step *= 2
    return x


def _block_diag(xb, nblk):
    head = lax.broadcasted_iota(jnp.int32, xb.shape, 1) // DN_DK
    zero = jnp.zeros_like(xb)
    return jnp.concatenate([jnp.where(head == g, xb, zero) for g in range(nblk)], axis=0)


def _split3_packed(x):
    t1 = x.astype(BF16).astype(F32)
    r1 = x - t1
    t2 = r1.astype(BF16).astype(F32)
    t3 = (r1 - t2).astype(BF16).astype(F32)
    return (t1 + pltpu.roll(t2, GB_USED, 1) + pltpu.roll(t3, 2 * GB_USED, 1)).astype(BF16)


def _gdn_kernel(q_ref, k_ref, v_ref, gb_ref, grow_ref, expand_ref, o_ref, s_scr, *, reverse):
    t = pl.program_id(1)
    rows = q_ref.shape[0]
    chunk = GDN_CHUNK
    nchunk = rows // chunk
    gw = GDN_GROUP * DN_DK
    pw = GDN_PAIR * DN_DK
    ngroup = DN_HEADS // GDN_GROUP
    npair = GDN_GROUP // GDN_PAIR

    @pl.when(t == 0)
    def _():
        s_scr[...] = jnp.zeros_like(s_scr)

    gb = gb_ref[...]
    lane = lax.broadcasted_iota(jnp.int32, gb.shape, 1)
    scal = jnp.where(lane < 2 * DN_HEADS, _segmented_cumsum(gb, 0, chunk, reverse), gb)
    spread = jnp.dot(_split3_packed(scal), expand_ref[...], preferred_element_type=F32)
    gx = spread[:, :DN_K]
    bx = spread[:, DN_K:]
    egx = jnp.exp(gx)

    ri = lax.broadcasted_iota(jnp.int32, (chunk, gw), 0)
    cj = lax.broadcasted_iota(jnp.int32, (chunk, gw), 1) % chunk
    incl = (ri <= cj) if reverse else (ri >= cj)
    strict = (ri < cj) if reverse else (ri > cj)
    eye = (ri == cj).astype(F32)
    pair_diag = (lax.broadcasted_iota(jnp.int32, (pw, pw), 0) // DN_DK
                 == lax.broadcasted_iota(jnp.int32, (pw, pw), 1) // DN_DK)
    last = 0 if reverse else chunk - 1
    order = list(range(nchunk - 1, -1, -1)) if reverse else list(range(nchunk))
    insts = [(c, g) for c in order for g in range(ngroup)]
    nt_dims = (((1,), (1,)), ((), ()))
    tn_dims = (((0,), (0,)), ((), ()))
    grow_base = ngroup if reverse else 0

    def rsl(c):
        return slice(c * chunk, (c + 1) * chunk)

    def gsl(g):
        return slice(g * gw, (g + 1) * gw)

    st = {key: {} for key in insts}

    for c, g in insts:
        d = st[c, g]
        kb = k_ref[rsl(c), gsl(g)]
        qb = q_ref[rsl(c), gsl(g)]
        kq = lax.dot_general(jnp.concatenate([kb, qb], axis=0), _block_diag(kb, GDN_GROUP), nt_dims,
                             preferred_element_type=F32)
        d["kf"] = kb.astype(F32)
        d["qf"] = qb.astype(F32)
        d["kk"] = kq[:chunk]
        d["qk"] = kq[chunk:]

    for c, g in insts:
        d = st[c, g]
        gx4 = gx[rsl(c), gsl(g)]
        grow = _segmented_cumsum(grow_ref[c], 1, chunk, reverse)[grow_base + g:grow_base + g + 1, :]
        decay = jnp.where(incl, jnp.exp(jnp.where(incl, gx4 - grow, 0.0)), 0.0)
        d["gx"] = gx4
        d["decay"] = decay
        d["neg_a"] = jnp.where(strict, -(bx[rsl(c), gsl(g)] * d.pop("kk") * decay), 0.0)

    for c, g in insts:
        d = st[c, g]
        d["inv"] = eye + d["neg_a"]
        pb = d.pop("neg_a").astype(BF16)
        d["power"] = jnp.dot(pb, _block_diag(pb, GDN_GROUP), preferred_element_type=F32)
    step = 4
    while step < chunk:
        for c, g in insts:
            d = st[c, g]
            pb = d["power"].astype(BF16)
            both = jnp.dot(jnp.concatenate([d["inv"].astype(BF16), pb], axis=0),
                           _block_diag(pb, GDN_GROUP), preferred_element_type=F32)
            d["inv"] = d["inv"] + both[:chunk]
            d["power"] = both[chunk:]
        step *= 2
    for c, g in insts:
        d = st[c, g]
        pb = d.pop("power").astype(BF16)
        d["inv"] = d["inv"] + jnp.dot(d["inv"].astype(BF16), _block_diag(pb, GDN_GROUP),
                                      preferred_element_type=F32)

    for c, g in insts:
        d = st[c, g]
        att = (d.pop("qk") * d.pop("decay")).astype(BF16)
        d["inv"] = d["inv"].astype(BF16)
        d["att_inv"] = jnp.dot(att, _block_diag(d["inv"], GDN_GROUP),
                               preferred_element_type=F32).astype(BF16)

    for c, g in insts:
        d = st[c, g]
        b4 = bx[rsl(c), gsl(g)]
        eg4 = egx[rsl(c), gsl(g)]
        yk = ((b4 * eg4) * d["kf"]).astype(BF16)
        yv = (b4 * v_ref[rsl(c), gsl(g)].astype(F32)).astype(BF16)
        rhs = jnp.concatenate([_block_diag(yk, GDN_GROUP), _block_diag(yv, GDN_GROUP)], axis=1)
        both = jnp.dot(jnp.concatenate([d.pop("inv"), d.pop("att_inv")], axis=0), rhs,
                       preferred_element_type=F32)
        d["w"] = both[:chunk, :gw].astype(BF16)
        d["u"] = both[:chunk, gw:].astype(BF16)
        d["q_eff"] = (d.pop("qf") * eg4 - both[chunk:, :gw]).astype(BF16)
        d["o0"] = both[chunk:, gw:]

    for c, g in insts:
        d = st[c, g]
        g_end = d["gx"][last:last + 1, :]
        kd = (d.pop("kf") * jnp.exp(g_end - d.pop("gx"))).astype(BF16)
        d["gl"] = jnp.exp(g_end)
        d["m"] = []
        d["b"] = []
        for p in range(npair):
            psl = slice(p * pw, (p + 1) * pw)
            z = lax.dot_general(kd[:, psl], jnp.concatenate([d["w"][:, psl], d["u"][:, psl]], axis=1),
                                tn_dims, preferred_element_type=F32)
            d["m"].append(jnp.where(pair_diag, z[:, :pw], 0.0).astype(BF16))
            d["b"].append(jnp.where(pair_diag, z[:, pw:], 0.0))

    states = [s_scr[i] for i in range(ngroup * npair)]
    for c in order:
        for g in range(ngroup):
            d = st[c, g]
            for p in range(npair):
                i = g * npair + p
                psl = slice(p * pw, (p + 1) * pw)
                state = states[i]
                res = jnp.dot(jnp.concatenate([d["q_eff"][:, psl], d["m"][p]], axis=0),
                              state.astype(BF16), preferred_element_type=F32)
                o_ref[rsl(c), g * gw + p * pw:g * gw + (p + 1) * pw] = res[:chunk] + d["o0"][:, psl]
                states[i] = d["gl"][:, psl] * state - res[chunk:] + d["b"][p]
    for i, state in enumerate(states):
        s_scr[i] = state


def _gdn_expand_matrix(reverse):
    dcol = DN_HEADS if reverse else 0
    bcol = 2 * DN_HEADS + dcol
    e = np.zeros((GB_WIDTH, 2 * DN_K), np.float32)
    for term in range(3):
        for h in range(DN_HEADS):
            e[term * GB_USED + dcol + h, h * DN_DK:(h + 1) * DN_DK] = 1.0
            e[term * GB_USED + bcol + h, DN_K + h * DN_DK:DN_K + (h + 1) * DN_DK] = 1.0
    return jnp.asarray(e, dtype=BF16)


def _gdn_scan(qn, kn, vn, gb, grow, reverse):
    bsz, s, _ = qn.shape
    rows = min(GDN_ROWS, s)
    nt = s // rows
    nchunk = rows // GDN_CHUNK
    expand = _gdn_expand_matrix(reverse)

    def tmap(t):
        return nt - 1 - t if reverse else t

    def tile(width):
        return pl.BlockSpec((None, rows, width), lambda b, t: (b, tmap(t), 0))

    grow_spec = pl.BlockSpec((None, nchunk) + grow.shape[2:], lambda b, t: (b, tmap(t), 0, 0))
    return pl.pallas_call(
        functools.partial(_gdn_kernel, reverse=reverse),
        grid=(bsz, nt),
        in_specs=[tile(DN_K), tile(DN_K), tile(DN_V), tile(GB_WIDTH), grow_spec,
                  _const_spec(expand.shape)],
        out_specs=tile(DN_V),
        out_shape=jax.ShapeDtypeStruct((bsz, s, DN_V), F32),
        scratch_shapes=[pltpu.VMEM((DN_HEADS // GDN_PAIR, GDN_PAIR * DN_DK, GDN_PAIR * DN_DV), F32)],
        compiler_params=_params(2),
        name="gdn_bwd" if reverse else "gdn_fwd",
    )(qn, kn, vn, gb, grow, expand)


def _decay_rows(gb):
    bsz, s, _ = gb.shape
    ngroup = DN_HEADS // GDN_GROUP
    g = gb[:, :, :2 * DN_HEADS].reshape(bsz, s // GDN_CHUNK, GDN_CHUNK, 2, ngroup, GDN_GROUP)
    g = jnp.transpose(g, (0, 1, 3, 4, 5, 2)).reshape(bsz, s // GDN_CHUNK, 2 * ngroup,
                                                     GDN_GROUP * GDN_CHUNK)
    return jnp.pad(g, ((0, 0), (0, 0), (0, SUBLANES - 2 * ngroup), (0, 0)))


def _mix_kernel(x_ref, ao_ref, of_ref, ob_ref, z_ref, nw_ref, watt_ref, wdn_ref, wgate_ref,
                bgate_ref, wo_ref, g_ref, b_ref, ones_ref, o_ref, mix_scr):
    x = x_ref[...]
    xb = x.astype(BF16)
    o = of_ref[...] + ob_ref[...]
    ms = _head_sums(o * o, ones_ref) * (1.0 / DN_DV)
    z = z_ref[...].astype(F32)
    dn = (o * lax.rsqrt(ms + RMS_EPS) * nw_ref[...] * (z * _sigmoid(z))).astype(BF16)
    ao = ao_ref[...]
    for c in range(0, D_MODEL, MIX_COLS):
        cols = slice(c, c + MIX_COLS)
        gcols = slice(D_MODEL + c, D_MODEL + c + MIX_COLS)
        y_att = jnp.dot(ao, watt_ref[:, cols], preferred_element_type=F32)
        y_dn = jnp.dot(dn, wdn_ref[:, cols], preferred_element_type=F32)
        g_att = _sigmoid(jnp.dot(xb, wgate_ref[:, cols], preferred_element_type=F32) + bgate_ref[:, cols])
        g_dn = _sigmoid(jnp.dot(xb, wgate_ref[:, gcols], preferred_element_type=F32) + bgate_ref[:, gcols])
        mix_scr[:, cols] = (g_att * y_att + g_dn * y_dn).astype(BF16)
    m = jnp.dot(mix_scr[...], wo_ref[...], preferred_element_type=F32)
    o_ref[...] = _layer_norm_rows(ALPHA * x + m, g_ref[...], b_ref[...])


def _mix(x2d, ao, of, ob, z, nw, watt, wdn, wgate, bgate, wo, g, b, ones):
    n = x2d.shape[0]
    rows = min(MIX_ROWS, n)

    def tile(width):
        return pl.BlockSpec((rows, width), lambda i: (i, 0))

    consts = (nw, watt, wdn, wgate, bgate, wo, g, b, ones)
    return pl.pallas_call(
        _mix_kernel,
        grid=(n // rows,),
        in_specs=[tile(D_MODEL), tile(ATT_Q), tile(DN_V), tile(DN_V), tile(DN_V)]
                 + [_const_spec(c.shape) for c in consts],
        out_specs=tile(D_MODEL),
        out_shape=jax.ShapeDtypeStruct((n, D_MODEL), F32),
        scratch_shapes=[pltpu.VMEM((rows, D_MODEL), BF16)],
        compiler_params=_params(1),
        name="mix_out",
    )(x2d, ao, of, ob, z, *consts)


def _t5_buckets(rel):
    half = REL_BUCKETS // 2
    ret = (rel > 0).astype(np.int32) * half
    n = np.abs(rel)
    max_exact = half // 2
    large = max_exact + (np.log(np.maximum(n, 1) / max_exact) / np.log(REL_MAX_DIST / max_exact)
                         * (half - max_exact)).astype(np.int32)
    large = np.minimum(large, half - 1)
    return (ret + np.where(n < max_exact, n, large)).astype(np.int32)


def _band_bias(rel_bias):
    i = np.arange(BLOCK)[:, None]
    j = np.arange(3 * BLOCK)[None, :]
    rel = j - BLOCK - i
    select = jnp.asarray(_t5_buckets(rel).reshape(-1, 1) == np.arange(REL_BUCKETS)[None, :], dtype=F32)
    bias = jnp.dot(select, rel_bias.astype(F32), precision=lax.Precision.HIGHEST)
    bias = jnp.transpose(bias.reshape(BLOCK, 3 * BLOCK, ATT_HEADS), (2, 0, 1))
    return jnp.where(jnp.asarray(np.abs(rel) <= WINDOW)[None], bias, MASKED)


def _head_block_ones():
    idx = np.arange(MXU_DIM) // DN_DK
    return jnp.asarray(idx[:, None] == idx[None, :], dtype=BF16)


def _row(v):
    return v.reshape(1, -1).astype(F32)


def _pad_lanes(v, width):
    v = v.reshape(1, -1).astype(F32)
    return jnp.pad(v, ((0, 0), (0, width - v.shape[1])))


def _trunk(x, layers, bias, ones):
    bsz, s, _ = x.shape
    x2d = x.reshape(bsz * s, D_MODEL)
    for lp in layers:
        x2d = _ffn_ln(x2d, *lp["ffn0"])
        x3d = x2d.reshape(bsz, s, D_MODEL)
        qa, ka, va, qn, kn, vn, z, gb = _proj(x3d, *lp["proj"], ones)
        ao = _attention(qa, ka, va, bias, lp["sink"])
        grow = _decay_rows(gb)
        o_fwd = _gdn_scan(qn, kn, vn, gb, grow, reverse=False)
        o_bwd = _gdn_scan(qn, kn, vn, gb, grow, reverse=True)
        flat = lambda a: a.reshape(bsz * s, a.shape[-1])
        x2d = _mix(x2d, flat(ao), flat(o_fwd), flat(o_bwd), flat(z), *lp["mix"], ones)
        x2d = _ffn_ln(x2d, *lp["ffn1"])
    return x2d.reshape(bsz, s, D_MODEL)


def kernel(x_prompt, x_sample, ln_g, ln_b, ffn_w_gate, ffn_w_up, ffn_w_down, w_in, attn_sink, rel_bias,
           dn_conv, dn_a_log, dn_dt_bias, dn_norm_w, w_att_out, w_dn_out, w_gate, b_gate, w_o):
    bias = _band_bias(rel_bias)
    ones = _head_block_ones()
    c_att = ATT_Q + 2 * ATT_KV
    c_dn = c_att + DN_QKV
    c_z = c_dn + DN_V
    layers = []
    for l in range(DEPTH):
        wi = w_in[l]
        wab = jnp.pad(wi[:, c_z:], ((0, 0), (0, GB_WIDTH - 4 * DN_HEADS))).astype(BF16)
        nexpa = _pad_lanes(-jnp.exp(dn_a_log[l].astype(F32)), GB_WIDTH)
        dtb = _pad_lanes(dn_dt_bias[l], GB_WIDTH)
        layers.append({
            "ffn0": (ffn_w_gate[l, 0].astype(BF16), ffn_w_up[l, 0].astype(BF16),
                     ffn_w_down[l, 0].astype(BF16), _row(ln_g[l, 0]), _row(ln_b[l, 0])),
            "ffn1": (ffn_w_gate[l, 1].astype(BF16), ffn_w_up[l, 1].astype(BF16),
                     ffn_w_down[l, 1].astype(BF16), _row(ln_g[l, 2]), _row(ln_b[l, 2])),
            "proj": (wi[:, :c_att].astype(BF16), wi[:, c_att:c_dn].astype(BF16),
                     wi[:, c_dn:c_z].astype(BF16), wab, dn_conv[l].astype(F32), nexpa, dtb),
            "sink": attn_sink[l].astype(F32),
            "mix": (_row(jnp.tile(dn_norm_w[l], DN_HEADS)), w_att_out[l].astype(BF16),
                    w_dn_out[l].astype(BF16), w_gate[l].astype(BF16), _row(b_gate[l]),
                    w_o[l].astype(BF16), _row(ln_g[l, 1]), _row(ln_b[l, 1])),
        })
    y_prompt = _trunk(x_prompt, layers, bias, ones)
    y_sample = _trunk(x_sample, layers, bias, ones)
    return (y_prompt, y_sample)
```
